```python
import math
import jax, jax.numpy as jnp
from jax import lax
import numpy as np

D_MODEL = 1024
BATCH = 16
SEQ = 256
DEPTH = 2
DEC_BATCH = 4
DEC_SEQ = 4096
PAST_LEN = 256

GRID_W = 64
EXPAND = 2
D_INNER = EXPAND * D_MODEL
A_WIDTH = D_INNER // 2
A_HEAD = 64
A_HEADS = A_WIDTH // A_HEAD
DECAY_LORA = 64
ICLR_LORA = 64
B_WIDTH = D_INNER // 2
B_HD = 64
B_HEADS = B_WIDTH // (2 * B_HD)
B_QK = B_HEADS * 2 * B_HD
C_WIDTH = D_INNER
CONV_W = 3
N_EVEN = (DEPTH + 1) // 2
N_ODD = DEPTH // 2
A_SHIFT_SIZES = (A_WIDTH, A_WIDTH, A_WIDTH, DECAY_LORA, DECAY_LORA, ICLR_LORA, ICLR_LORA)
A_SHIFT_COLS = sum(A_SHIFT_SIZES)
EVEN_SIZES = (A_SHIFT_COLS, A_WIDTH, B_QK, B_QK, B_WIDTH, B_WIDTH)
EVEN_IN = sum(EVEN_SIZES)
ODD_IN = 4 * C_WIDTH
Q_BLOCK = 128
ROPE_BASE = 10000.0
NORM_EPS = 1e-6
GN_EPS = 64e-5

kernel_name = 'hybrid_rwkv7_diffattn_shortconv_diffusion_step'


def _split(x, sizes):
    return jnp.split(x, [int(i) for i in np.cumsum(sizes)[:-1]], axis=-1)


def _rms(x, w):
    xf = x.astype(jnp.float32)
    y = xf * lax.rsqrt(jnp.mean(xf * xf, axis=-1, keepdims=True) + NORM_EPS)
    return (y * w.astype(jnp.float32)).astype(x.dtype)


def _ada(cvec, w, b):
    m = jax.nn.silu(cvec) @ w + b
    shift, scale, gate = jnp.split(m, 3, axis=-1)
    return shift[:, None, :], scale[:, None, :], gate[:, None, :]


def _centred_shift(p):
    zero = jnp.zeros_like(p[:, :1])
    prev = jnp.concatenate([zero, p[:, :-1]], axis=1)
    nxt = jnp.concatenate([p[:, 1:], zero], axis=1)
    return 0.5 * (prev + nxt)


def _axial_rope_angles(n_rows):
    row = jnp.repeat(jnp.arange(n_rows, dtype=jnp.float32), GRID_W)
    col = jnp.tile(jnp.arange(GRID_W, dtype=jnp.float32), n_rows)
    half = B_HD // 2
    inv = ROPE_BASE ** (-jnp.arange(0, half, 2, dtype=jnp.float32) / half)
    ang_row = (row[:, None] * inv)[:, None, None, :]
    ang_col = (col[:, None] * inv)[:, None, None, :]
    return ang_row, ang_col


def _rope_chunk(x, ang):
    x1, x2 = jnp.split(x, 2, axis=-1)
    cos, sin = jnp.cos(ang), jnp.sin(ang)
    return jnp.concatenate([x1 * cos - x2 * sin, x2 * cos + x1 * sin], axis=-1)


def _apply_axial_rope(x, ang_row, ang_col):
    xr, xc = jnp.split(x.astype(jnp.float32), 2, axis=-1)
    return jnp.concatenate([_rope_chunk(xr, ang_row), _rope_chunk(xc, ang_col)], axis=-1).astype(x.dtype)


def _rwkv_scan(r, w, k, v, kk, a, s0, reverse):
    def step(S, inp):
        r_t, w_t, k_t, v_t, kk_t, a_t = inp
        s_kk = jnp.einsum('bhvk,bhk->bhv', S, kk_t)
        S = (S * w_t[:, :, None, :] - s_kk[..., None] * (kk_t * a_t)[:, :, None, :]
             + v_t[..., None] * k_t[:, :, None, :])
        return S, jnp.einsum('bhvk,bhk->bhv', S, r_t)
    xs = tuple(jnp.moveaxis(t, 1, 0) for t in (r, w, k, v, kk, a))
    s_fin, o = lax.scan(step, s0, xs, reverse=reverse)
    return jnp.moveaxis(o, 0, 1), s_fin


def _rwkv_branch(pa, ga, s0_f, s0_b, mu, w0, w_up, a0, a_up, k_k, k_a, r_k, lnx_w, lnx_b):
    bsz, t_len, _ = pa.shape
    f32 = jnp.float32
    heads = lambda t: t.reshape(bsz, t_len, A_HEADS, A_HEAD)
    p = pa.astype(f32)
    p = p + mu * (_centred_shift(p) - p)
    r, k, v, wl_f, wl_b, al_f, al_b = _split(p, A_SHIFT_SIZES)
    kk = heads(k * k_k)
    kk = kk / jnp.maximum(jnp.sqrt(jnp.sum(kk * kk, axis=-1, keepdims=True)), 1e-12)
    r_h, v_h = heads(r), heads(v)
    outs, bonuses, states = [], [], []
    for d, (wl, al, s0) in enumerate(((wl_f, al_f, s0_f), (wl_b, al_b, s0_b))):
        wlog = -jax.nn.softplus(-(w0[d] + jnp.tanh(wl) @ w_up[d])) - 0.5
        decay = jnp.exp(-jnp.exp(wlog))
        a = jax.nn.sigmoid(a0[d] + al @ a_up[d])
        k_d = heads(k * (1.0 + (a - 1.0) * k_a))
        o_d, s_d = _rwkv_scan(r_h, heads(decay), k_d, v_h, kk, heads(a), s0.astype(f32), reverse=(d == 1))
        outs.append(o_d)
        bonuses.append(jnp.sum(r_h * k_d * r_k, axis=-1, keepdims=True) * v_h)
        states.append(s_d.astype(pa.dtype))
    o = outs[0] + outs[1]
    mean = jnp.mean(o, axis=-1, keepdims=True)
    var = jnp.mean(jnp.square(o - mean), axis=-1, keepdims=True)
    o = ((o - mean) * lax.rsqrt(var + GN_EPS) * lnx_w.reshape(A_HEADS, A_HEAD)
         + lnx_b.reshape(A_HEADS, A_HEAD))
    y = (o + bonuses[0] + bonuses[1]).reshape(bsz, t_len, A_WIDTH).astype(ga.dtype) * jax.nn.silu(ga)
    return y, states[0], states[1]


def _diff_attend_block(qb, k, v, lam):
    s = jnp.einsum('bqhcd,bkhcd->bhcqk', qb, k) * (B_HD ** -0.5)
    p = jax.nn.softmax(s, axis=-1)
    attn = p[:, :, 0] - lam * p[:, :, 1]
    return jnp.einsum('bhqk,bkhe->bqhe', attn, v)


def _diff_attend(q, k, v, lam):
    bsz, t_len = q.shape[:2]
    nb = t_len // Q_BLOCK
    qb = jnp.moveaxis(q.reshape(bsz, nb, Q_BLOCK, B_HEADS, 2, B_HD), 1, 0)
    o = lax.map(lambda blk: _diff_attend_block(blk, k, v, lam), qb)
    return jnp.moveaxis(o, 0, 1).reshape(bsz, t_len, B_HEADS, 2 * B_HD)


def _diff_branch(pq, pk, pv, gb, q_norm, k_norm, lam_vec, subln_w, lam_init, rope, ctx_k, ctx_v):
    bsz, t_len, _ = pq.shape
    f32 = jnp.float32
    q = _rms(pq.reshape(bsz, t_len, B_HEADS, 2, B_HD), q_norm)
    k = _rms(pk.reshape(bsz, t_len, B_HEADS, 2, B_HD), k_norm)
    v = pv.reshape(bsz, t_len, B_HEADS, 2 * B_HD)
    if rope is None:
        keys, vals = k, v
    else:
        ang_row, ang_col = rope
        q = _apply_axial_rope(q, ang_row, ang_col)
        keys = jnp.concatenate([_apply_axial_rope(k, ang_row, ang_col), ctx_k.astype(k.dtype)], axis=1)
        vals = jnp.concatenate([v, ctx_v.astype(v.dtype)], axis=1)
    lf = lam_vec.astype(f32)
    lam = jnp.exp(jnp.sum(lf[0] * lf[1])) - jnp.exp(jnp.sum(lf[2] * lf[3])) + lam_init
    o = _diff_attend(q.astype(f32), keys.astype(f32), vals.astype(f32), lam)
    o = _rms(o, subln_w) * (1.0 - lam_init)
    y = o.reshape(bsz, t_len, B_WIDTH).astype(pq.dtype) * jax.nn.silu(gb)
    return y, k, v


def _even_mixer(h, rope, s0_f, s0_b, ctx_k, ctx_v, w_in, w_out, rw, dp):
    pa, ga, pq, pk, pv, gb = _split(h @ w_in, EVEN_SIZES)
    ya, s_f, s_b = _rwkv_branch(pa, ga, s0_f, s0_b, *rw)
    yb, k_c, v_c = _diff_branch(pq, pk, pv, gb, *dp, rope, ctx_k, ctx_v)
    return jnp.concatenate([ya, yb], axis=-1) @ w_out, s_f, s_b, k_c, v_c


def _dwconv3(u, w, b):
    pad = (CONV_W - 1) // 2
    out = lax.conv_general_dilated(u, w[:, None, :].astype(u.dtype), window_strides=(1,),
                                   padding=((pad, pad),), dimension_numbers=('NWC', 'WIO', 'NWC'),
                                   feature_group_count=u.shape[-1])
    return out + b


def _odd_mixer(h, w_in, conv_w, conv_b, w_out):
    bg, cg, u, z = jnp.split(h @ w_in, 4, axis=-1)
    y = bg * _dwconv3(cg * u, conv_w, conv_b) * jax.nn.silu(z)
    return y @ w_out


def setup_inputs(seed: int = 0) -> dict:
    key = jax.random.key(seed)
    ks = iter(jax.random.split(key, 40))
    nrm = lambda shape, s=1.0: s * jax.random.normal(next(ks), shape, jnp.float32)
    return {
        'x_prompt': nrm((BATCH, SEQ, D_MODEL)),
        'x_sample': nrm((DEC_BATCH, DEC_SEQ, D_MODEL)),
        'state_rwkv_fwd': nrm((DEC_BATCH, N_EVEN, A_HEADS, A_HEAD, A_HEAD)),
        'state_rwkv_bwd': nrm((DEC_BATCH, N_EVEN, A_HEADS, A_HEAD, A_HEAD)),
        'cache_diff_k': nrm((DEC_BATCH, N_EVEN, PAST_LEN, B_HEADS, 2, B_HD)),
        'cache_diff_v': nrm((DEC_BATCH, N_EVEN, PAST_LEN, B_HEADS, 2 * B_HD)),
        'c': nrm((DEC_BATCH, D_MODEL)),
        'c_ctx': nrm((D_MODEL,)),
        'norm_w': 1.0 + nrm((DEPTH, D_MODEL), 0.1),
        'ada_w': nrm((DEPTH, D_MODEL, 3 * D_MODEL), 0.5 * D_MODEL ** -0.5),
        'ada_b': nrm((DEPTH, 3 * D_MODEL), 0.02),
        'e_w_in': nrm((N_EVEN, D_MODEL, EVEN_IN), D_MODEL ** -0.5),
        'e_w_out': nrm((N_EVEN, A_WIDTH + B_WIDTH, D_MODEL), (A_WIDTH + B_WIDTH) ** -0.5),
        'e_mu': jax.random.uniform(next(ks), (N_EVEN, A_SHIFT_COLS), jnp.float32),
        'e_w0': -2.0 + nrm((N_EVEN, 2, A_WIDTH), 0.5),
        'e_w_up': nrm((N_EVEN, 2, DECAY_LORA, A_WIDTH), 0.1),
        'e_a0': nrm((N_EVEN, 2, A_WIDTH), 0.1),
        'e_a_up': nrm((N_EVEN, 2, ICLR_LORA, A_WIDTH), 0.1),
        'e_k_k': 0.85 + nrm((N_EVEN, A_WIDTH), 0.05),
        'e_k_a': 1.0 + nrm((N_EVEN, A_WIDTH), 0.05),
        'e_r_k': nrm((N_EVEN, A_HEADS, A_HEAD), 0.1),
        'e_lnx_w': 1.0 + nrm((N_EVEN, A_WIDTH), 0.1),
        'e_lnx_b': nrm((N_EVEN, A_WIDTH), 0.02),
        'e_q_norm': 1.0 + nrm((N_EVEN, B_HD), 0.1),
        'e_k_norm': 1.0 + nrm((N_EVEN, B_HD), 0.1),
        'e_lambda': nrm((N_EVEN, 4, B_HD), 0.1),
        'e_subln': 1.0 + nrm((N_EVEN, 2 * B_HD), 0.1),
        'o_w_in': nrm((N_ODD, D_MODEL, ODD_IN), D_MODEL ** -0.5),
        'o_conv_w': nrm((N_ODD, CONV_W, C_WIDTH), CONV_W ** -0.5),
        'o_conv_b': nrm((N_ODD, C_WIDTH), 0.02),
        'o_w_out': nrm((N_ODD, C_WIDTH, D_MODEL), C_WIDTH ** -0.5),
    }


def reference(x_prompt, x_sample, state_rwkv_fwd, state_rwkv_bwd, cache_diff_k, cache_diff_v, c, c_ctx,
              norm_w, ada_w, ada_b,
              e_w_in, e_w_out, e_mu, e_w0, e_w_up, e_a0, e_a_up, e_k_k, e_k_a, e_r_k, e_lnx_w, e_lnx_b,
              e_q_norm, e_k_norm, e_lambda, e_subln,
              o_w_in, o_conv_w, o_conv_b, o_w_out):
    n_rows = x_sample.shape[1] // GRID_W
    rope = _axial_rope_angles(n_rows)
    xp, xs = x_prompt, x_sample
    new_sf, new_sb, new_k, new_v = [], [], [], []
    for layer in range(DEPTH):
        sh_p, sc_p, g_p = _ada(c_ctx[None, :], ada_w[layer], ada_b[layer])
        sh_s, sc_s, g_s = _ada(c, ada_w[layer], ada_b[layer])
        hp = _rms(xp, norm_w[layer]) * (1.0 + sc_p) + sh_p
        hs = _rms(xs, norm_w[layer]) * (1.0 + sc_s) + sh_s
        i = layer // 2
        if layer % 2 == 0:
            lam_init = 0.8 - 0.6 * math.exp(-0.3 * layer)
            rw = (e_mu[i], e_w0[i], e_w_up[i], e_a0[i], e_a_up[i], e_k_k[i], e_k_a[i], e_r_k[i],
                  e_lnx_w[i], e_lnx_b[i])
            dp = (e_q_norm[i], e_k_norm[i], e_lambda[i], e_subln[i], lam_init)
            zeros = jnp.zeros((xp.shape[0], A_HEADS, A_HEAD, A_HEAD), jnp.float32)
            op, s_f, s_b, k_c, v_c = _even_mixer(hp, None, zeros, zeros, None, None,
                                                 e_w_in[i], e_w_out[i], rw, dp)
            os_, _, _, _, _ = _even_mixer(hs, rope, state_rwkv_fwd[:, i], state_rwkv_bwd[:, i],
                                          cache_diff_k[:, i], cache_diff_v[:, i],
                                          e_w_in[i], e_w_out[i], rw, dp)
            new_sf.append(s_f)
            new_sb.append(s_b)
            new_k.append(k_c)
            new_v.append(v_c)
        else:
            op = _odd_mixer(hp, o_w_in[i], o_conv_w[i], o_conv_b[i], o_w_out[i])
            os_ = _odd_mixer(hs, o_w_in[i], o_conv_w[i], o_conv_b[i], o_w_out[i])
        xp = xp + (g_p * op).astype(xp.dtype)
        xs = xs + (g_s * os_).astype(xs.dtype)
    return (xp, xs, jnp.stack(new_sf, axis=1), jnp.stack(new_sb, axis=1),
            jnp.stack(new_k, axis=1), jnp.stack(new_v, axis=1))
```

```python
import functools
import math

import jax
import jax.numpy as jnp
from jax import lax
from jax.experimental import pallas as pl
from jax.experimental.pallas import tpu as pltpu

F32 = jnp.float32
BF16 = jnp.bfloat16
HI = lax.Precision.HIGHEST

D_MODEL = 1024
A_WIDTH = 1024
A_HEAD = 64
A_HEADS = A_WIDTH // A_HEAD
LORA = 64
B_HD = 64
B_HEADS = 8
B_WIDTH = 1024
C_WIDTH = 2048
GRID_W = 64
ROPE_BASE = 10000.0
NORM_EPS = 1e-6
GN_EPS = 64e-5
MOD_ROWS = 8
CHUNK = 64
HALO = 8
VMEM_LIMIT = 48 * 1024 * 1024

_NT = (((1,), (1,)), ((), ()))
_TN = (((0,), (0,)), ((), ()))


def _dot(x, y, dims=None, precision=None):
    if dims is None:
        return jnp.dot(x, y, precision=precision, preferred_element_type=F32)
    return lax.dot_general(x, y, dims, precision=precision, preferred_element_type=F32)


def _params(*sem):
    return pltpu.CompilerParams(dimension_semantics=sem, vmem_limit_bytes=VMEM_LIMIT)


def _ada_kernel(c_ref, w_ref, b_ref, o_ref):
    cv = c_ref[...]
    s = cv * jax.nn.sigmoid(cv)
    o_ref[0] = _dot(s, w_ref[0], precision=HI) + b_ref[0]


def _ada(cvec, ada_w, ada_b):
    depth = ada_w.shape[0]
    tn = D_MODEL
    out = pl.pallas_call(
        _ada_kernel,
        grid=(depth, 3 * D_MODEL // tn),
        in_specs=[pl.BlockSpec((MOD_ROWS, D_MODEL), lambda l, j: (0, 0)),
                  pl.BlockSpec((1, D_MODEL, tn), lambda l, j: (l, 0, j)),
                  pl.BlockSpec((1, 1, tn), lambda l, j: (l, 0, j))],
        out_specs=pl.BlockSpec((1, MOD_ROWS, tn), lambda l, j: (l, 0, j)),
        out_shape=jax.ShapeDtypeStruct((depth, MOD_ROWS, 3 * D_MODEL), F32),
        compiler_params=_params("parallel", "parallel"),
        name="ada",
    )(cvec, ada_w, ada_b.reshape(depth, 1, 3 * D_MODEL))
    return out.reshape(depth * MOD_ROWS, 1, 3 * D_MODEL)


def _in_proj_kernel(x_ref, nw_ref, mod_ref, w_ref, o_ref, h_ref):
    @pl.when(pl.program_id(2) == 0)
    def _():
        x = x_ref[0]
        ms = jnp.mean(x * x, axis=-1, keepdims=True)
        y = x * lax.rsqrt(ms + NORM_EPS) * nw_ref[...]
        m = mod_ref[0]
        h = y * (1.0 + m[:, D_MODEL:2 * D_MODEL]) + m[:, :D_MODEL]
        h_ref[...] = h.astype(BF16)

    o_ref[0] = _dot(h_ref[...], w_ref[...])


def _in_proj(x, norm_w, mods, mod_base, mod_stride, w, tm):
    bsz, t_len, _ = x.shape
    n_out = w.shape[1]
    tn = next(c for c in (1024, 768, 512, 256, 128) if n_out % c == 0)
    return pl.pallas_call(
        _in_proj_kernel,
        grid=(bsz, t_len // tm, n_out // tn),
        in_specs=[pl.BlockSpec((1, tm, D_MODEL), lambda b, i, j: (b, i, 0)),
                  pl.BlockSpec((1, D_MODEL), lambda b, i, j: (0, 0)),
                  pl.BlockSpec((1, 1, 3 * D_MODEL), lambda b, i, j: (mod_base + mod_stride * b, 0, 0)),
                  pl.BlockSpec((D_MODEL, tn), lambda b, i, j: (0, j))],
        out_specs=pl.BlockSpec((1, tm, tn), lambda b, i, j: (b, i, j)),
        out_shape=jax.ShapeDtypeStruct((bsz, t_len, n_out), F32),
        scratch_shapes=[pltpu.VMEM((tm, D_MODEL), BF16)],
        compiler_params=_params("parallel", "parallel", "arbitrary"),
        name="in_proj",
    )(x, norm_w, mods, w)


def _qk_prep_kernel(*refs, rope, emit_cache):
    pq_ref, pk_ref, pv_ref, qn_ref, kn_ref = refs[:5]
    pos = 5
    if rope:
        cos_ref, sin_ref = refs[pos:pos + 2]
        pos += 2
    q_out, k_out, v_out = refs[pos:pos + 3]
    kc_out = refs[pos + 3] if emit_cache else None
    width = 2 * B_HD
    tm = pq_ref.shape[1]
    lane = lax.broadcasted_iota(jnp.int32, (tm, width), 1)
    low = lane < B_HD
    first16 = (lane % 32) < 16

    def norm(x, w):
        ss = x * x
        s0 = jnp.sum(jnp.where(low, ss, 0.0), axis=-1, keepdims=True)
        s1 = jnp.sum(jnp.where(low, 0.0, ss), axis=-1, keepdims=True)
        ms = jnp.where(low, s0, s1) * (1.0 / B_HD)
        return x * lax.rsqrt(ms + NORM_EPS) * w

    def rot(x):
        if not rope:
            return x
        partner = jnp.where(first16, pltpu.roll(x, width - 16, 1), pltpu.roll(x, 16, 1))
        return x * cos_ref[...] + partner * sin_ref[...]

    for h in range(B_HEADS):
        sl = slice(h * width, (h + 1) * width)
        q = norm(pq_ref[0, :, sl], qn_ref[...])
        k = norm(pk_ref[0, :, sl], kn_ref[...])
        if emit_cache:
            kc_out[0, :, sl] = k
        q_out[0, :, sl] = (rot(q) * (B_HD ** -0.5)).astype(BF16)
        k_out[0, :, sl] = rot(k).astype(BF16)
    v_out[0] = pv_ref[0].astype(BF16)


def _qk_prep(proj, q_norm2, k_norm2, cos, sin, tm, emit_cache):
    bsz, t_len, _ = proj.shape
    rope = cos is not None
    col = lambda c: pl.BlockSpec((1, tm, B_WIDTH), lambda b, i: (b, i, c))
    vec = pl.BlockSpec((1, 2 * B_HD), lambda b, i: (0, 0))
    in_specs = [col(4), col(5), col(6), vec, vec]
    args = [proj, proj, proj, q_norm2, k_norm2]
    if rope:
        tab = pl.BlockSpec((tm, 2 * B_HD), lambda b, i: (i, 0))
        in_specs += [tab, tab]
        args += [cos, sin]
    out_spec = pl.BlockSpec((1, tm, B_WIDTH), lambda b, i: (b, i, 0))
    out_shape = [jax.ShapeDtypeStruct((bsz, t_len, B_WIDTH), BF16)] * 3
    out_specs = [out_spec] * 3
    if emit_cache:
        out_shape = out_shape + [jax.ShapeDtypeStruct((bsz, t_len, B_WIDTH), F32)]
        out_specs = out_specs + [out_spec]
    return pl.pallas_call(
        functools.partial(_qk_prep_kernel, rope=rope, emit_cache=emit_cache),
        grid=(bsz, t_len // tm),
        in_specs=in_specs, out_specs=out_specs, out_shape=out_shape,
        compiler_params=_params("parallel", "parallel"),
        name="qk_prep",
    )(*args)


def _attn_kernel(*refs, lam_init, has_ctx):
    lam_ref, q_ref, k_ref, v_ref = refs[:4]
    pos = 4
    if has_ctx:
        ck_ref, cv_ref = refs[pos:pos + 2]
        pos += 2
    gb_ref, sub_ref, o_ref = refs[pos:pos + 3]
    q = q_ref[0]
    lane = lax.broadcasted_iota(jnp.int32, q.shape, 1)
    zero = jnp.zeros_like(q)
    k = k_ref[0]
    v = v_ref[0]

    def component(qc):
        s = _dot(qc, k, _NT)
        m = jnp.max(s, axis=-1, keepdims=True)
        if has_ctx:
            sc = _dot(qc, ck_ref[0], _NT)
            m = jnp.maximum(m, jnp.max(sc, axis=-1, keepdims=True))
        p = jnp.exp(s - m)
        den = jnp.sum(p, axis=-1, keepdims=True)
        acc = _dot(p.astype(BF16), v)
        if has_ctx:
            pc = jnp.exp(sc - m)
            den = den + jnp.sum(pc, axis=-1, keepdims=True)
            acc = acc + _dot(pc.astype(BF16), cv_ref[0])
        return acc / den

    lv = lam_ref[...]
    lam = (jnp.exp(jnp.sum(lv[0:1] * lv[1:2], axis=-1, keepdims=True))
           - jnp.exp(jnp.sum(lv[2:3] * lv[3:4], axis=-1, keepdims=True)) + lam_init)
    o = component(jnp.where(lane < B_HD, q, zero)) - lam * component(jnp.where(lane < B_HD, zero, q))
    ms = jnp.mean(o * o, axis=-1, keepdims=True)
    o = o * lax.rsqrt(ms + NORM_EPS) * sub_ref[...] * (1.0 - lam_init)
    gb = gb_ref[0]
    o_ref[0] = (o * (gb * jax.nn.sigmoid(gb))).astype(BF16)


def _attention(lam_vec, q, k, v, ctx_k, ctx_v, proj, subln, lam_init, tq):
    bsz, t_len, _ = q.shape
    has_ctx = ctx_k is not None
    width = 2 * B_HD
    qspec = pl.BlockSpec((1, tq, width), lambda b, h, i: (b, i, h))
    kspec = pl.BlockSpec((1, t_len, width), lambda b, h, i: (b, 0, h))
    in_specs = [pl.BlockSpec((4, B_HD), lambda b, h, i: (0, 0)), qspec, kspec, kspec]
    args = [lam_vec, q, k, v]
    if has_ctx:
        cspec = pl.BlockSpec((1, ctx_k.shape[1], width), lambda b, h, i: (b, 0, h))
        in_specs += [cspec, cspec]
        args += [ctx_k, ctx_v]
    in_specs += [pl.BlockSpec((1, tq, width), lambda b, h, i: (b, i, 7 * B_HEADS + h)),
                 pl.BlockSpec((1, width), lambda b, h, i: (0, 0))]
    args += [proj, subln]
    return pl.pallas_call(
        functools.partial(_attn_kernel, lam_init=lam_init, has_ctx=has_ctx),
        grid=(bsz, B_HEADS, t_len // tq),
        in_specs=in_specs, out_specs=qspec,
        out_shape=jax.ShapeDtypeStruct((bsz, t_len, B_WIDTH), BF16),
        compiler_params=_params("parallel", "parallel", "parallel"),
        name="diff_attn",
    )(*args)


def _rwkv_kernel(rkv_ref, rkv_p_ref, rkv_n_ref, lo_ref, lo_p_ref, lo_n_ref, mu_rkv_ref, mu_lo_ref,
                 w0_ref, wup_ref, a0_ref, aup_ref, kk_ref, ka_ref, rk_ref, s0_ref,
                 o_ref, bon_ref, s_ref,
                 kk_s, pex_s, apinv_s, kp_s, r_s, v_s, rkd_s, ptot_s, *, n_chunks):
    C = CHUNK
    d = pl.program_id(1)
    c = pl.program_id(2)
    cc = c + d * (n_chunks - 1 - 2 * c)
    first = cc == 0
    last = cc == n_chunks - 1
    row = lax.broadcasted_iota(jnp.int32, (C, 1), 0)

    def token_shift(x, x_prev, x_next, mu):
        prev_row = jnp.where(first, 0.0, x_prev[HALO - 1:HALO, :])
        next_row = jnp.where(last, 0.0, x_next[0:1, :])
        xp = jnp.where(row == 0, prev_row, pltpu.roll(x, 1, 0))
        xn = jnp.where(row == C - 1, next_row, pltpu.roll(x, C - 1, 0))
        return x + mu * (0.5 * (xp + xn) - x)

    p = token_shift(rkv_ref[0], rkv_p_ref[0], rkv_n_ref[0], mu_rkv_ref[...])
    lo = token_shift(lo_ref[0], lo_p_ref[0], lo_n_ref[0], mu_lo_ref[...])
    r = p[:, :A_WIDTH]
    k = p[:, A_WIDTH:2 * A_WIDTH]
    v = p[:, 2 * A_WIDTH:]
    fwd = d == 0
    wl = jnp.where(fwd, lo[:, 0:LORA], lo[:, LORA:2 * LORA])
    al = jnp.where(fwd, lo[:, 2 * LORA:3 * LORA], lo[:, 3 * LORA:])
    xw = w0_ref[0] + _dot(jnp.tanh(wl), wup_ref[0], precision=HI)
    logw = -math.exp(-0.5) * jax.nn.sigmoid(xw)
    a = jax.nn.sigmoid(a0_ref[0] + _dot(al, aup_ref[0], precision=HI))
    kd = k * (1.0 + (a - 1.0) * ka_ref[...])

    ti = lax.broadcasted_iota(jnp.int32, (C, C), 0)
    si = lax.broadcasted_iota(jnp.int32, (C, C), 1)
    before = (ti - si) * (1 - 2 * d)
    strict = before > 0
    incl = before >= 0
    cum = _dot(jnp.where(incl, 1.0, 0.0), logw, precision=HI)
    kk_s[...] = k * kk_ref[...]
    pex_s[...] = jnp.exp(cum - logw)
    pinv = jnp.exp(-cum)
    apinv_s[...] = a * pinv
    kp_s[...] = kd * pinv
    r_s[...] = r * jnp.exp(cum)
    v_s[...] = v
    rkd_s[...] = r * kd * rk_ref[...]
    ptot_s[...] = jnp.broadcast_to(jnp.exp(jnp.sum(logw, axis=0, keepdims=True)), (HALO, A_WIDTH))

    @pl.when(c == 0)
    def _():
        s_ref[...] = s0_ref[...]

    eye = jnp.where(ti == si, 1.0, 0.0)
    for h in range(A_HEADS):
        sl = slice(h * A_HEAD, (h + 1) * A_HEAD)
        kk = kk_s[:, sl]
        kk = kk / jnp.maximum(jnp.sqrt(jnp.sum(kk * kk, axis=-1, keepdims=True)), 1e-12)
        am = kk * pex_s[:, sl]
        bm = kk * apinv_s[:, sl]
        kp = kp_s[:, sl]
        rm = r_s[:, sl]
        vm = v_s[:, sl]
        ar = jnp.concatenate([am, rm], axis=0)
        xb = _dot(ar, bm, _NT, HI)
        xk = _dot(ar, kp, _NT, HI)
        nab = jnp.where(strict, -xb[:C], 0.0)
        mak = jnp.where(strict, xk[:C], 0.0)
        mrb = jnp.where(incl, xb[C:], 0.0)
        mrk = jnp.where(incl, xk[C:], 0.0)
        tinv = eye + nab
        pw = nab
        for _ in range(int(math.log2(C)) - 1):
            pw = _dot(pw, pw, precision=HI)
            tinv = tinv + _dot(tinv, pw, precision=HI)
        wm = _dot(tinv, am, precision=HI)
        u0 = _dot(tinv, _dot(mak, vm, precision=HI), precision=HI)
        rp = rm - _dot(mrb, wm, precision=HI)
        o0 = _dot(mrk, vm, precision=HI) - _dot(mrb, u0, precision=HI)
        pt = ptot_s[0:1, sl]
        g = (eye - _dot(wm, bm, _TN, HI)) * pt
        hm = (_dot(vm, kp, _TN, HI) - _dot(u0, bm, _TN, HI)) * pt
        s = s_ref[0, 0, h]
        o_ref[0, 0, :, sl] = _dot(rp, s, _NT, HI) + o0
        s_ref[0, 0, h] = _dot(s, g, precision=HI) + hm
        bon_ref[0, 0, :, sl] = jnp.sum(rkd_s[:, sl], axis=-1, keepdims=True) * vm


def _rwkv(proj, s0, mu_rkv, mu_lo, w0, w_up, a0, a_up, k_k, k_a, r_k):
    bsz, t_len, _ = proj.shape
    C = CHUNK
    n_chunks = t_len // C
    n_halo = t_len // HALO
    per = C // HALO
    chunk = lambda d, c: c + d * (n_chunks - 1 - 2 * c)
    rkv_w = 3 * A_WIDTH
    lo_w = 4 * LORA
    lo_col = (8 * D_MODEL) // lo_w

    def main(width, col):
        return pl.BlockSpec((1, C, width), lambda b, d, c: (b, chunk(d, c), col))

    def prev(width, col):
        return pl.BlockSpec((1, HALO, width), lambda b, d, c: (b, jnp.maximum(chunk(d, c) * per - 1, 0), col))

    def nxt(width, col):
        return pl.BlockSpec((1, HALO, width),
                            lambda b, d, c: (b, jnp.minimum((chunk(d, c) + 1) * per, n_halo - 1), col))

    vec = lambda width: pl.BlockSpec((1, width), lambda b, d, c: (0, 0))
    dvec = pl.BlockSpec((1, 1, A_WIDTH), lambda b, d, c: (d, 0, 0))
    dmat = pl.BlockSpec((1, LORA, A_WIDTH), lambda b, d, c: (d, 0, 0))
    sspec = pl.BlockSpec((1, 1, A_HEADS, A_HEAD, A_HEAD), lambda b, d, c: (d, b, 0, 0, 0))
    ospec = pl.BlockSpec((1, 1, C, A_WIDTH), lambda b, d, c: (d, b, chunk(d, c), 0))
    big = jax.ShapeDtypeStruct((2, bsz, t_len, A_WIDTH), F32)
    return pl.pallas_call(
        functools.partial(_rwkv_kernel, n_chunks=n_chunks),
        grid=(bsz, 2, n_chunks),
        in_specs=[main(rkv_w, 0), prev(rkv_w, 0), nxt(rkv_w, 0),
                  main(lo_w, lo_col), prev(lo_w, lo_col), nxt(lo_w, lo_col),
                  vec(rkv_w), vec(lo_w), dvec, dmat, dvec, dmat, vec(A_WIDTH), vec(A_WIDTH), vec(A_WIDTH), sspec],
        out_specs=[ospec, ospec, sspec],
        out_shape=[big, big, jax.ShapeDtypeStruct((2, bsz, A_HEADS, A_HEAD, A_HEAD), F32)],
        scratch_shapes=[pltpu.VMEM((C, A_WIDTH), F32)] * 7 + [pltpu.VMEM((HALO, A_WIDTH), F32)],
        compiler_params=_params("parallel", "arbitrary", "arbitrary"),
        name="rwkv",
    )(proj, proj, proj, proj, proj, proj, mu_rkv, mu_lo,
      w0.reshape(2, 1, A_WIDTH), w_up, a0.reshape(2, 1, A_WIDTH), a_up, k_k, k_a, r_k, s0)


def _even_out_kernel(of_ref, ob_ref, bf_ref, bb_ref, ga_ref, yb_ref, x_ref, mod_ref, lw_ref, lb_ref, w_ref,
                     o_ref, y_s):
    o = of_ref[0, 0] + ob_ref[0, 0]
    bon = bf_ref[0, 0] + bb_ref[0, 0]
    ga = ga_ref[0]
    gate_a = ga * jax.nn.sigmoid(ga)
    for h in range(A_HEADS):
        sl = slice(h * A_HEAD, (h + 1) * A_HEAD)
        oh = o[:, sl]
        mean = jnp.mean(oh, axis=-1, keepdims=True)
        cen = oh - mean
        var = jnp.mean(cen * cen, axis=-1, keepdims=True)
        yh = cen * lax.rsqrt(var + GN_EPS) * lw_ref[:, sl] + lb_ref[:, sl]
        y_s[:, sl] = ((yh + bon[:, sl]) * gate_a[:, sl]).astype(BF16)
    y_s[:, A_WIDTH:] = yb_ref[0]
    out = _dot(y_s[...], w_ref[...])
    gate = mod_ref[0][:, 2 * D_MODEL:]
    o_ref[0] = x_ref[0] + gate * out


def _even_out(o, bon, proj, yb, x, mods, mod_base, mod_stride, lnx_w, lnx_b, w_out, tm):
    bsz, t_len, _ = x.shape
    dspec = lambda d: pl.BlockSpec((1, 1, tm, A_WIDTH), lambda b, i: (d, b, i, 0))
    row = lambda col: pl.BlockSpec((1, tm, D_MODEL), lambda b, i: (b, i, col))
    vec = pl.BlockSpec((1, A_WIDTH), lambda b, i: (0, 0))
    return pl.pallas_call(
        _even_out_kernel,
        grid=(bsz, t_len // tm),
        in_specs=[dspec(0), dspec(1), dspec(0), dspec(1), row(3), row(0), row(0),
                  pl.BlockSpec((1, 1, 3 * D_MODEL), lambda b, i: (mod_base + mod_stride * b, 0, 0)),
                  vec, vec, pl.BlockSpec((A_WIDTH + B_WIDTH, D_MODEL), lambda b, i: (0, 0))],
        out_specs=row(0),
        out_shape=jax.ShapeDtypeStruct(x.shape, F32),
        scratch_shapes=[pltpu.VMEM((tm, A_WIDTH + B_WIDTH), BF16)],
        compiler_params=_params("parallel", "parallel"),
        name="even_out",
    )(o, o, bon, bon, proj, yb, x, mods, lnx_w, lnx_b, w_out)


def _odd_out_kernel(bg_ref, cg_ref, u_ref, z_ref, cgp_ref, up_ref, cgn_ref, un_ref, cw_ref, cb_ref,
                    x_ref, mod_ref, w_ref, o_ref):
    i = pl.program_id(1)
    tm = bg_ref.shape[1]
    row = lax.broadcasted_iota(jnp.int32, (tm, 1), 0)
    cu = cg_ref[0] * u_ref[0]
    prev_row = jnp.where(i == 0, 0.0, cgp_ref[0][HALO - 1:HALO, :] * up_ref[0][HALO - 1:HALO, :])
    next_row = jnp.where(i == pl.num_programs(1) - 1, 0.0, cgn_ref[0][0:1, :] * un_ref[0][0:1, :])
    cu_prev = jnp.where(row == 0, prev_row, pltpu.roll(cu, 1, 0))
    cu_next = jnp.where(row == tm - 1, next_row, pltpu.roll(cu, tm - 1, 0))
    cw = cw_ref[...]
    conv = cu_prev * cw[0:1] + cu * cw[1:2] + cu_next * cw[2:3] + cb_ref[...]
    z = z_ref[0]
    y = bg_ref[0] * conv * (z * jax.nn.sigmoid(z))
    out = _dot(y.astype(BF16), w_ref[...])
    gate = mod_ref[0][:, 2 * D_MODEL:]
    o_ref[0] = x_ref[0] + gate * out


def _odd_out(proj, x, mods, mod_base, mod_stride, conv_w, conv_b, w_out, tm):
    bsz, t_len, _ = x.shape
    per = tm // HALO
    n_halo = t_len // HALO
    main = lambda col: pl.BlockSpec((1, tm, C_WIDTH), lambda b, i: (b, i, col))
    prev = lambda col: pl.BlockSpec((1, HALO, C_WIDTH), lambda b, i: (b, jnp.maximum(i * per - 1, 0), col))
    nxt = lambda col: pl.BlockSpec((1, HALO, C_WIDTH),
                                   lambda b, i: (b, jnp.minimum((i + 1) * per, n_halo - 1), col))
    xspec = pl.BlockSpec((1, tm, D_MODEL), lambda b, i: (b, i, 0))
    return pl.pallas_call(
        _odd_out_kernel,
        grid=(bsz, t_len // tm),
        in_specs=[main(0), main(1), main(2), main(3), prev(1), prev(2), nxt(1), nxt(2),
                  pl.BlockSpec(conv_w.shape, lambda b, i: (0, 0)),
                  pl.BlockSpec((1, C_WIDTH), lambda b, i: (0, 0)),
                  xspec,
                  pl.BlockSpec((1, 1, 3 * D_MODEL), lambda b, i: (mod_base + mod_stride * b, 0, 0)),
                  pl.BlockSpec((C_WIDTH, D_MODEL), lambda b, i: (0, 0))],
        out_specs=xspec,
        out_shape=jax.ShapeDtypeStruct(x.shape, F32),
        compiler_params=_params("parallel", "parallel"),
        name="odd_out",
    )(proj, proj, proj, proj, proj, proj, proj, proj, conv_w, conv_b, x, mods, w_out)


def _rope_tables(t_len):
    pos = jnp.arange(t_len, dtype=jnp.int32)
    row = (pos // GRID_W).astype(F32)
    col = (pos % GRID_W).astype(F32)
    half = B_HD // 2
    inv = ROPE_BASE ** (-jnp.arange(0, half, 2, dtype=F32) / half)
    ar = row[:, None] * inv
    ac = col[:, None] * inv
    cos = jnp.concatenate([jnp.cos(ar), jnp.cos(ar), jnp.cos(ac), jnp.cos(ac)], axis=-1)
    sin = jnp.concatenate([-jnp.sin(ar), jnp.sin(ar), -jnp.sin(ac), jnp.sin(ac)], axis=-1)
    return jnp.tile(cos, (1, 2)), jnp.tile(sin, (1, 2))


def _proj_rows(x, norm_w, mods, mod_base, mod_stride, w, flatten):
    bsz, t_len, _ = x.shape
    if flatten:
        x = x.reshape(1, bsz * t_len, D_MODEL)
    rows = x.shape[1]
    tm = next(c for c in (1024, 512, 256, 128) if rows % c == 0)
    out = _in_proj(x, norm_w, mods, mod_base, mod_stride, w, tm)
    return out.reshape(bsz, t_len, -1)


def _even_layer(x, mods, mod_base, mod_stride, flatten, norm_w, w_in, w_out, rw, dp, lam_init,
                s0, ctx_k, ctx_v, rope, emit_cache):
    bsz, t_len, _ = x.shape
    proj = _proj_rows(x, norm_w, mods, mod_base, mod_stride, w_in, flatten)
    mu_rkv, mu_lo, w0, w_up, a0, a_up, k_k, k_a, r_k, lnx_w, lnx_b = rw
    q_norm2, k_norm2, lam_vec, subln = dp
    tm = min(t_len, 256)
    cos, sin = rope if rope is not None else (None, None)
    prep = _qk_prep(proj, q_norm2, k_norm2, cos, sin, tm, emit_cache)
    qn, kn, vn = prep[:3]
    yb = _attention(lam_vec, qn, kn, vn, ctx_k, ctx_v, proj, subln, lam_init, tm)
    o, bon, s_new = _rwkv(proj, s0, mu_rkv, mu_lo, w0, w_up, a0, a_up, k_k, k_a, r_k)
    x_new = _even_out(o, bon, proj, yb, x, mods, mod_base, mod_stride, lnx_w, lnx_b, w_out, tm)
    k_cache = prep[3] if emit_cache else None
    v_cache = proj[:, :, 6 * D_MODEL:7 * D_MODEL] if emit_cache else None
    return x_new, s_new, k_cache, v_cache


def _odd_layer(x, mods, mod_base, mod_stride, flatten, norm_w, w_in, conv_w, conv_b, w_out):
    proj = _proj_rows(x, norm_w, mods, mod_base, mod_stride, w_in, flatten)
    tm = min(x.shape[1], 256)
    return _odd_out(proj, x, mods, mod_base, mod_stride, conv_w, conv_b, w_out, tm)


def kernel(x_prompt, x_sample, state_rwkv_fwd, state_rwkv_bwd, cache_diff_k, cache_diff_v, c, c_ctx,
           norm_w, ada_w, ada_b,
           e_w_in, e_w_out, e_mu, e_w0, e_w_up, e_a0, e_a_up, e_k_k, e_k_a, e_r_k, e_lnx_w, e_lnx_b,
           e_q_norm, e_k_norm, e_lambda, e_subln,
           o_w_in, o_conv_w, o_conv_b, o_w_out):
    depth = norm_w.shape[0]
    bsz, seq = x_prompt.shape[:2]
    dec_b, dec_t = x_sample.shape[:2]
    past = cache_diff_k.shape[2]
    assert dec_b + 1 <= MOD_ROWS
    cvec = jnp.concatenate([c_ctx[None, :], c, jnp.zeros((MOD_ROWS - 1 - dec_b, D_MODEL), F32)], axis=0)
    mods = _ada(cvec, ada_w, ada_b)
    rope = _rope_tables(dec_t)
    rkv_w = 3 * A_WIDTH
    lora_end = rkv_w + 4 * LORA

    xp, xs = x_prompt, x_sample
    new_sf, new_sb, new_k, new_v = [], [], [], []
    for layer in range(depth):
        i = layer // 2
        nw = norm_w[layer][None, :]
        base = layer * MOD_ROWS
        if layer % 2 == 0:
            lam_init = 0.8 - 0.6 * math.exp(-0.3 * layer)
            w_in = jnp.concatenate([e_w_in[i][:, :rkv_w], e_w_in[i][:, lora_end:], e_w_in[i][:, rkv_w:lora_end]],
                                   axis=1).astype(BF16)
            w_out = e_w_out[i].astype(BF16)
            row = lambda t: t.reshape(1, -1)
            rw = (row(e_mu[i][:rkv_w]), row(e_mu[i][rkv_w:lora_end]), e_w0[i], e_w_up[i], e_a0[i], e_a_up[i],
                  row(e_k_k[i]), row(e_k_a[i]), row(e_r_k[i]), row(e_lnx_w[i]), row(e_lnx_b[i]))
            dp = (jnp.tile(e_q_norm[i], 2)[None, :], jnp.tile(e_k_norm[i], 2)[None, :], e_lambda[i],
                  row(e_subln[i]))
            zeros = jnp.zeros((2, bsz, A_HEADS, A_HEAD, A_HEAD), F32)
            xp, s_p, k_c, v_c = _even_layer(xp, mods, base, 0, True, nw, w_in, w_out, rw, dp, lam_init,
                                            zeros, None, None, None, True)
            s0 = jnp.stack([state_rwkv_fwd[:, i], state_rwkv_bwd[:, i]], axis=0)
            ctx_k = cache_diff_k[:, i].reshape(dec_b, past, B_WIDTH).astype(BF16)
            ctx_v = cache_diff_v[:, i].reshape(dec_b, past, B_WIDTH).astype(BF16)
            xs, _, _, _ = _even_layer(xs, mods, base + 1, 1, False, nw, w_in, w_out, rw, dp, lam_init,
                                      s0, ctx_k, ctx_v, rope, False)
            new_sf.append(s_p[0])
            new_sb.append(s_p[1])
            new_k.append(k_c.reshape(bsz, seq, B_HEADS, 2, B_HD))
            new_v.append(v_c.reshape(bsz, seq, B_HEADS, 2 * B_HD))
        else:
            w_in = o_w_in[i].astype(BF16)
            w_out = o_w_out[i].astype(BF16)
            cb = o_conv_b[i][None, :]
            xp = _odd_layer(xp, mods, base, 0, True, nw, w_in, o_conv_w[i], cb, w_out)
            xs = _odd_layer(xs, mods, base + 1, 1, False, nw, w_in, o_conv_w[i], cb, w_out)
    return (xp, xs, jnp.stack(new_sf, axis=1), jnp.stack(new_sb, axis=1),
            jnp.stack(new_k, axis=1), jnp.stack(new_v, axis=1))
```

```python
import functools
import math

import jax
import jax.numpy as jnp
from jax import lax
from jax.experimental import pallas as pl
from jax.experimental.pallas import tpu as pltpu

F32 = jnp.float32
BF16 = jnp.bfloat16
HI = lax.Precision.HIGHEST
PH = None

D_MODEL = 1024
A_WIDTH = 1024
A_HEAD = 64
A_HEADS = A_WIDTH // A_HEAD
LORA = 64
B_HD = 64
B_HEADS = 8
B_WIDTH = 1024
C_WIDTH = 2048
GRID_W = 64
ROPE_BASE = 10000.0
NORM_EPS = 1e-6
GN_EPS = 64e-5
MOD_ROWS = 8
CHUNK = 64
HALO = 8
VMEM_LIMIT = 48 * 1024 * 1024

_NT = (((1,), (1,)), ((), ()))
_TN = (((0,), (0,)), ((), ()))


def _dot(x, y, dims=None, precision=None):
    if dims is None:
        return jnp.dot(x, y, precision=precision, preferred_element_type=F32)
    return lax.dot_general(x, y, dims, precision=precision, preferred_element_type=F32)


def _params(*sem):
    return pltpu.CompilerParams(dimension_semantics=sem, vmem_limit_bytes=VMEM_LIMIT)


def _ada_kernel(c_ref, w_ref, b_ref, o_ref):
    cv = c_ref[...]
    s = cv * jax.nn.sigmoid(cv)
    o_ref[0] = _dot(s, w_ref[0], precision=HI) + b_ref[0]


def _ada(cvec, ada_w, ada_b):
    depth = ada_w.shape[0]
    tn = D_MODEL
    out = pl.pallas_call(
        _ada_kernel,
        grid=(depth, 3 * D_MODEL // tn),
        in_specs=[pl.BlockSpec((MOD_ROWS, D_MODEL), lambda l, j: (0, 0)),
                  pl.BlockSpec((1, D_MODEL, tn), lambda l, j: (l, 0, j)),
                  pl.BlockSpec((1, 1, tn), lambda l, j: (l, 0, j))],
        out_specs=pl.BlockSpec((1, MOD_ROWS, tn), lambda l, j: (l, 0, j)),
        out_shape=jax.ShapeDtypeStruct((depth, MOD_ROWS, 3 * D_MODEL), F32),
        compiler_params=_params("parallel", "parallel"),
        name="ada",
    )(cvec, ada_w, ada_b.reshape(depth, 1, 3 * D_MODEL))
    return out.reshape(depth * MOD_ROWS, 1, 3 * D_MODEL)


def _in_proj_kernel(x_ref, nw_ref, mod_ref, w_ref, o_ref, h_ref):
    @pl.when(pl.program_id(2) == 0)
    def _():
        x = x_ref[0]
        ms = jnp.mean(x * x, axis=-1, keepdims=True)
        y = x * lax.rsqrt(ms + NORM_EPS) * nw_ref[...]
        m = mod_ref[0]
        h = y * (1.0 + m[:, D_MODEL:2 * D_MODEL]) + m[:, :D_MODEL]
        h_ref[...] = h.astype(BF16)

    o_ref[0] = _dot(h_ref[...], w_ref[...])


def _in_proj(x, norm_w, mods, mod_base, mod_stride, w, tm):
    bsz, t_len, _ = x.shape
    n_out = w.shape[1]
    tn = next(c for c in (1024, 768, 512, 256, 128) if n_out % c == 0)
    return pl.pallas_call(
        _in_proj_kernel,
        grid=(bsz, t_len // tm, n_out // tn),
        in_specs=[pl.BlockSpec((1, tm, D_MODEL), lambda b, i, j: (b, i, 0)),
                  pl.BlockSpec((1, D_MODEL), lambda b, i, j: (0, 0)),
                  pl.BlockSpec((1, 1, 3 * D_MODEL), lambda b, i, j: (mod_base + mod_stride * b, 0, 0)),
                  pl.BlockSpec((D_MODEL, tn), lambda b, i, j: (0, j))],
        out_specs=pl.BlockSpec((1, tm, tn), lambda b, i, j: (b, i, j)),
        out_shape=jax.ShapeDtypeStruct((bsz, t_len, n_out), F32),
        scratch_shapes=[pltpu.VMEM((tm, D_MODEL), BF16)],
        compiler_params=_params("parallel", "parallel", "arbitrary"),
        name="in_proj",
    )(x, norm_w, mods, w)


def _qk_prep_kernel(*refs, rope, emit_cache):
    pq_ref, pk_ref, pv_ref, qn_ref, kn_ref = refs[:5]
    pos = 5
    if rope:
        cos_ref, sin_ref = refs[pos:pos + 2]
        pos += 2
    q_out, k_out, v_out = refs[pos:pos + 3]
    kc_out = refs[pos + 3] if emit_cache else None
    width = 2 * B_HD
    tm = pq_ref.shape[1]
    lane = lax.broadcasted_iota(jnp.int32, (tm, width), 1)
    low = lane < B_HD
    first16 = (lane % 32) < 16

    def norm(x, w):
        ss = x * x
        s0 = jnp.sum(jnp.where(low, ss, 0.0), axis=-1, keepdims=True)
        s1 = jnp.sum(jnp.where(low, 0.0, ss), axis=-1, keepdims=True)
        ms = jnp.where(low, s0, s1) * (1.0 / B_HD)
        return x * lax.rsqrt(ms + NORM_EPS) * w

    def rot(x):
        if not rope:
            return x
        partner = jnp.where(first16, pltpu.roll(x, width - 16, 1), pltpu.roll(x, 16, 1))
        return x * cos_ref[...] + partner * sin_ref[...]

    for h in range(B_HEADS):
        sl = slice(h * width, (h + 1) * width)
        q = norm(pq_ref[0, :, sl], qn_ref[...])
        k = norm(pk_ref[0, :, sl], kn_ref[...])
        if emit_cache:
            kc_out[0, :, sl] = k
        q_out[0, :, sl] = (rot(q) * (B_HD ** -0.5)).astype(BF16)
        k_out[0, :, sl] = rot(k).astype(BF16)
    v_out[0] = pv_ref[0].astype(BF16)


def _qk_prep(proj, q_norm2, k_norm2, cos, sin, tm, emit_cache):
    bsz, t_len, _ = proj.shape
    rope = cos is not None
    col = lambda c: pl.BlockSpec((1, tm, B_WIDTH), lambda b, i: (b, i, c))
    vec = pl.BlockSpec((1, 2 * B_HD), lambda b, i: (0, 0))
    in_specs = [col(4), col(5), col(6), vec, vec]
    args = [proj, proj, proj, q_norm2, k_norm2]
    if rope:
        tab = pl.BlockSpec((tm, 2 * B_HD), lambda b, i: (i, 0))
        in_specs += [tab, tab]
        args += [cos, sin]
    out_spec = pl.BlockSpec((1, tm, B_WIDTH), lambda b, i: (b, i, 0))
    out_shape = [jax.ShapeDtypeStruct((bsz, t_len, B_WIDTH), BF16)] * 3
    out_specs = [out_spec] * 3
    if emit_cache:
        out_shape = out_shape + [jax.ShapeDtypeStruct((bsz, t_len, B_WIDTH), F32)]
        out_specs = out_specs + [out_spec]
    return pl.pallas_call(
        functools.partial(_qk_prep_kernel, rope=rope, emit_cache=emit_cache),
        grid=(bsz, t_len // tm),
        in_specs=in_specs, out_specs=out_specs, out_shape=out_shape,
        compiler_params=_params("parallel", "parallel"),
        name="qk_prep",
    )(*args)


def _attn_kernel(*refs, lam_init, has_ctx):
    lam_ref, q_ref, k_ref, v_ref = refs[:4]
    pos = 4
    if has_ctx:
        ck_ref, cv_ref = refs[pos:pos + 2]
        pos += 2
    gb_ref, sub_ref, o_ref = refs[pos:pos + 3]
    q = q_ref[0]
    lane = lax.broadcasted_iota(jnp.int32, q.shape, 1)
    zero = jnp.zeros_like(q)
    k = k_ref[0]
    v = v_ref[0]

    def component(qc):
        s = _dot(qc, k, _NT)
        m = jnp.max(s, axis=-1, keepdims=True)
        if has_ctx:
            sc = _dot(qc, ck_ref[0], _NT)
            m = jnp.maximum(m, jnp.max(sc, axis=-1, keepdims=True))
        p = jnp.exp(s - m)
        den = jnp.sum(p, axis=-1, keepdims=True)
        acc = _dot(p.astype(BF16), v)
        if has_ctx:
            pc = jnp.exp(sc - m)
            den = den + jnp.sum(pc, axis=-1, keepdims=True)
            acc = acc + _dot(pc.astype(BF16), cv_ref[0])
        return acc / den

    lv = lam_ref[...]
    lam = (jnp.exp(jnp.sum(lv[0:1] * lv[1:2], axis=-1, keepdims=True))
           - jnp.exp(jnp.sum(lv[2:3] * lv[3:4], axis=-1, keepdims=True)) + lam_init)
    o = component(jnp.where(lane < B_HD, q, zero)) - lam * component(jnp.where(lane < B_HD, zero, q))
    ms = jnp.mean(o * o, axis=-1, keepdims=True)
    o = o * lax.rsqrt(ms + NORM_EPS) * sub_ref[...] * (1.0 - lam_init)
    gb = gb_ref[0]
    o_ref[0] = (o * (gb * jax.nn.sigmoid(gb))).astype(BF16)


def _attention(lam_vec, q, k, v, ctx_k, ctx_v, proj, subln, lam_init, tq):
    bsz, t_len, _ = q.shape
    has_ctx = ctx_k is not None
    width = 2 * B_HD
    qspec = pl.BlockSpec((1, tq, width), lambda b, h, i: (b, i, h))
    kspec = pl.BlockSpec((1, t_len, width), lambda b, h, i: (b, 0, h))
    in_specs = [pl.BlockSpec((4, B_HD), lambda b, h, i: (0, 0)), qspec, kspec, kspec]
    args = [lam_vec, q, k, v]
    if has_ctx:
        cspec = pl.BlockSpec((1, ctx_k.shape[1], width), lambda b, h, i: (b, 0, h))
        in_specs += [cspec, cspec]
        args += [ctx_k, ctx_v]
    in_specs += [pl.BlockSpec((1, tq, width), lambda b, h, i: (b, i, 7 * B_HEADS + h)),
                 pl.BlockSpec((1, width), lambda b, h, i: (0, 0))]
    args += [proj, subln]
    return pl.pallas_call(
        functools.partial(_attn_kernel, lam_init=lam_init, has_ctx=has_ctx),
        grid=(bsz, B_HEADS, t_len // tq),
        in_specs=in_specs, out_specs=qspec,
        out_shape=jax.ShapeDtypeStruct((bsz, t_len, B_WIDTH), BF16),
        compiler_params=_params("parallel", "parallel", "parallel"),
        name="diff_attn",
    )(*args)


def _rwkv_kernel(rkv_ref, rkv_p_ref, rkv_n_ref, lo_ref, lo_p_ref, lo_n_ref, mu_rkv_ref, mu_lo_ref,
                 w0_ref, wup_ref, a0_ref, aup_ref, kk_ref, ka_ref, rk_ref, s0_ref,
                 o_ref, bon_ref, s_ref,
                 kk_s, pex_s, apinv_s, kp_s, r_s, v_s, rkd_s, ptot_s, *, n_chunks):
    C = CHUNK
    d = pl.program_id(1)
    c = pl.program_id(2)
    cc = c + d * (n_chunks - 1 - 2 * c)
    first = cc == 0
    last = cc == n_chunks - 1
    row = lax.broadcasted_iota(jnp.int32, (C, 1), 0)

    def token_shift(x, x_prev, x_next, mu):
        prev_row = jnp.where(first, 0.0, x_prev[HALO - 1:HALO, :])
        next_row = jnp.where(last, 0.0, x_next[0:1, :])
        xp = jnp.where(row == 0, prev_row, pltpu.roll(x, 1, 0))
        xn = jnp.where(row == C - 1, next_row, pltpu.roll(x, C - 1, 0))
        return x + mu * (0.5 * (xp + xn) - x)

    p = token_shift(rkv_ref[0], rkv_p_ref[0], rkv_n_ref[0], mu_rkv_ref[...])
    lo = token_shift(lo_ref[0], lo_p_ref[0], lo_n_ref[0], mu_lo_ref[...])
    r = p[:, :A_WIDTH]
    k = p[:, A_WIDTH:2 * A_WIDTH]
    v = p[:, 2 * A_WIDTH:]
    fwd = d == 0
    wl = jnp.where(fwd, lo[:, 0:LORA], lo[:, LORA:2 * LORA])
    al = jnp.where(fwd, lo[:, 2 * LORA:3 * LORA], lo[:, 3 * LORA:])
    xw = w0_ref[0] + _dot(jnp.tanh(wl), wup_ref[0], precision=HI)
    logw = -math.exp(-0.5) * jax.nn.sigmoid(xw)
    a = jax.nn.sigmoid(a0_ref[0] + _dot(al, aup_ref[0], precision=HI))
    kd = k * (1.0 + (a - 1.0) * ka_ref[...])

    ti = lax.broadcasted_iota(jnp.int32, (C, C), 0)
    si = lax.broadcasted_iota(jnp.int32, (C, C), 1)
    before = (ti - si) * (1 - 2 * d)
    strict = before > 0
    incl = before >= 0
    cum = _dot(jnp.where(incl, 1.0, 0.0), logw, precision=HI)
    kk_s[...] = k * kk_ref[...]
    pex_s[...] = jnp.exp(cum - logw)
    pinv = jnp.exp(-cum)
    apinv_s[...] = a * pinv
    kp_s[...] = kd * pinv
    r_s[...] = r * jnp.exp(cum)
    v_s[...] = v
    rkd_s[...] = r * kd * rk_ref[...]
    ptot_s[...] = jnp.broadcast_to(jnp.exp(jnp.sum(logw, axis=0, keepdims=True)), (HALO, A_WIDTH))

    @pl.when(c == 0)
    def _():
        s_ref[...] = s0_ref[...]

    eye = jnp.where(ti == si, 1.0, 0.0)
    hs = range(A_HEADS)
    sls = [slice(h * A_HEAD, (h + 1) * A_HEAD) for h in hs]

    def each(fn, *lists):
        return [fn(*xs) for xs in zip(*lists)]

    def unit(x):
        return x / jnp.maximum(jnp.sqrt(jnp.sum(x * x, axis=-1, keepdims=True)), 1e-12)

    kk = [unit(kk_s[:, sl]) for sl in sls]
    am = [kk[h] * pex_s[:, sls[h]] for h in hs]
    bm = [kk[h] * apinv_s[:, sls[h]] for h in hs]
    kp = [kp_s[:, sl] for sl in sls]
    rm = [r_s[:, sl] for sl in sls]
    vm = [v_s[:, sl] for sl in sls]
    ar = each(lambda a, r: jnp.concatenate([a, r], axis=0), am, rm)
    xb = each(lambda x, y: _dot(x, y, _NT, PH), ar, bm)
    xk = each(lambda x, y: _dot(x, y, _NT, PH), ar, kp)
    nab = [jnp.where(strict, -x[:C], 0.0) for x in xb]
    mak = [jnp.where(strict, x[:C], 0.0) for x in xk]
    mrb = [jnp.where(incl, x[C:], 0.0) for x in xb]
    mrk = [jnp.where(incl, x[C:], 0.0) for x in xk]
    tinv = [eye + n for n in nab]
    pw = nab
    for _ in range(int(math.log2(C)) - 1):
        pw = each(lambda p: _dot(p, p, precision=PH), pw)
        tinv = each(lambda t, p: t + _dot(t, p, precision=PH), tinv, pw)
    wm = each(lambda t, a: _dot(t, a, precision=PH), tinv, am)
    mv = each(lambda m, v: _dot(m, v, precision=PH), mak, vm)
    u0 = each(lambda t, x: _dot(t, x, precision=PH), tinv, mv)
    rp = each(lambda r, m, w: r - _dot(m, w, precision=PH), rm, mrb, wm)
    o0 = each(lambda mk, v, mb, u: _dot(mk, v, precision=PH) - _dot(mb, u, precision=PH), mrk, vm, mrb, u0)
    pt = [ptot_s[0:1, sl] for sl in sls]
    g = each(lambda w, b, p: (eye - _dot(w, b, _TN, PH)) * p, wm, bm, pt)
    hm = each(lambda v, k, u, b, p: (_dot(v, k, _TN, PH) - _dot(u, b, _TN, PH)) * p, vm, kp, u0, bm, pt)
    for h in hs:
        s = s_ref[0, 0, h]
        o_ref[0, 0, :, sls[h]] = _dot(rp[h], s, _NT, PH) + o0[h]
        s_ref[0, 0, h] = _dot(s, g[h], precision=PH) + hm[h]
        bon_ref[0, 0, :, sls[h]] = jnp.sum(rkd_s[:, sls[h]], axis=-1, keepdims=True) * vm[h]


def _rwkv(proj, s0, mu_rkv, mu_lo, w0, w_up, a0, a_up, k_k, k_a, r_k):
    bsz, t_len, _ = proj.shape
    C = CHUNK
    n_chunks = t_len // C
    n_halo = t_len // HALO
    per = C // HALO
    chunk = lambda d, c: c + d * (n_chunks - 1 - 2 * c)
    rkv_w = 3 * A_WIDTH
    lo_w = 4 * LORA
    lo_col = (8 * D_MODEL) // lo_w

    def main(width, col):
        return pl.BlockSpec((1, C, width), lambda b, d, c: (b, chunk(d, c), col))

    def prev(width, col):
        return pl.BlockSpec((1, HALO, width), lambda b, d, c: (b, jnp.maximum(chunk(d, c) * per - 1, 0), col))

    def nxt(width, col):
        return pl.BlockSpec((1, HALO, width),
                            lambda b, d, c: (b, jnp.minimum((chunk(d, c) + 1) * per, n_halo - 1), col))

    vec = lambda width: pl.BlockSpec((1, width), lambda b, d, c: (0, 0))
    dvec = pl.BlockSpec((1, 1, A_WIDTH), lambda b, d, c: (d, 0, 0))
    dmat = pl.BlockSpec((1, LORA, A_WIDTH), lambda b, d, c: (d, 0, 0))
    sspec = pl.BlockSpec((1, 1, A_HEADS, A_HEAD, A_HEAD), lambda b, d, c: (d, b, 0, 0, 0))
    ospec = pl.BlockSpec((1, 1, C, A_WIDTH), lambda b, d, c: (d, b, chunk(d, c), 0))
    big = jax.ShapeDtypeStruct((2, bsz, t_len, A_WIDTH), F32)
    return pl.pallas_call(
        functools.partial(_rwkv_kernel, n_chunks=n_chunks),
        grid=(bsz, 2, n_chunks),
        in_specs=[main(rkv_w, 0), prev(rkv_w, 0), nxt(rkv_w, 0),
                  main(lo_w, lo_col), prev(lo_w, lo_col), nxt(lo_w, lo_col),
                  vec(rkv_w), vec(lo_w), dvec, dmat, dvec, dmat, vec(A_WIDTH), vec(A_WIDTH), vec(A_WIDTH), sspec],
        out_specs=[ospec, ospec, sspec],
        out_shape=[big, big, jax.ShapeDtypeStruct((2, bsz, A_HEADS, A_HEAD, A_HEAD), F32)],
        scratch_shapes=[pltpu.VMEM((C, A_WIDTH), F32)] * 7 + [pltpu.VMEM((HALO, A_WIDTH), F32)],
        compiler_params=_params("parallel", "arbitrary", "arbitrary"),
        name="rwkv",
    )(proj, proj, proj, proj, proj, proj, mu_rkv, mu_lo,
      w0.reshape(2, 1, A_WIDTH), w_up, a0.reshape(2, 1, A_WIDTH), a_up, k_k, k_a, r_k, s0)


def _even_out_kernel(of_ref, ob_ref, bf_ref, bb_ref, ga_ref, yb_ref, x_ref, mod_ref, lw_ref, lb_ref, w_ref,
                     o_ref, y_s):
    o = of_ref[0, 0] + ob_ref[0, 0]
    bon = bf_ref[0, 0] + bb_ref[0, 0]
    ga = ga_ref[0]
    gate_a = ga * jax.nn.sigmoid(ga)
    for h in range(A_HEADS):
        sl = slice(h * A_HEAD, (h + 1) * A_HEAD)
        oh = o[:, sl]
        mean = jnp.mean(oh, axis=-1, keepdims=True)
        cen = oh - mean
        var = jnp.mean(cen * cen, axis=-1, keepdims=True)
        yh = cen * lax.rsqrt(var + GN_EPS) * lw_ref[:, sl] + lb_ref[:, sl]
        y_s[:, sl] = ((yh + bon[:, sl]) * gate_a[:, sl]).astype(BF16)
    y_s[:, A_WIDTH:] = yb_ref[0]
    out = _dot(y_s[...], w_ref[...])
    gate = mod_ref[0][:, 2 * D_MODEL:]
    o_ref[0] = x_ref[0] + gate * out


def _even_out(o, bon, proj, yb, x, mods, mod_base, mod_stride, lnx_w, lnx_b, w_out, tm):
    bsz, t_len, _ = x.shape
    dspec = lambda d: pl.BlockSpec((1, 1, tm, A_WIDTH), lambda b, i: (d, b, i, 0))
    row = lambda col: pl.BlockSpec((1, tm, D_MODEL), lambda b, i: (b, i, col))
    vec = pl.BlockSpec((1, A_WIDTH), lambda b, i: (0, 0))
    return pl.pallas_call(
        _even_out_kernel,
        grid=(bsz, t_len // tm),
        in_specs=[dspec(0), dspec(1), dspec(0), dspec(1), row(3), row(0), row(0),
                  pl.BlockSpec((1, 1, 3 * D_MODEL), lambda b, i: (mod_base + mod_stride * b, 0, 0)),
                  vec, vec, pl.BlockSpec((A_WIDTH + B_WIDTH, D_MODEL), lambda b, i: (0, 0))],
        out_specs=row(0),
        out_shape=jax.ShapeDtypeStruct(x.shape, F32),
        scratch_shapes=[pltpu.VMEM((tm, A_WIDTH + B_WIDTH), BF16)],
        compiler_params=_params("parallel", "parallel"),
        name="even_out",
    )(o, o, bon, bon, proj, yb, x, mods, lnx_w, lnx_b, w_out)


def _odd_out_kernel(bg_ref, cg_ref, u_ref, z_ref, cgp_ref, up_ref, cgn_ref, un_ref, cw_ref, cb_ref,
                    x_ref, mod_ref, w_ref, o_ref):
    i = pl.program_id(1)
    tm = bg_ref.shape[1]
    row = lax.broadcasted_iota(jnp.int32, (tm, 1), 0)
    cu = cg_ref[0] * u_ref[0]
    prev_row = jnp.where(i == 0, 0.0, cgp_ref[0][HALO - 1:HALO, :] * up_ref[0][HALO - 1:HALO, :])
    next_row = jnp.where(i == pl.num_programs(1) - 1, 0.0, cgn_ref[0][0:1, :] * un_ref[0][0:1, :])
    cu_prev = jnp.where(row == 0, prev_row, pltpu.roll(cu, 1, 0))
    cu_next = jnp.where(row == tm - 1, next_row, pltpu.roll(cu, tm - 1, 0))
    cw = cw_ref[...]
    conv = cu_prev * cw[0:1] + cu * cw[1:2] + cu_next * cw[2:3] + cb_ref[...]
    z = z_ref[0]
    y = bg_ref[0] * conv * (z * jax.nn.sigmoid(z))
    out = _dot(y.astype(BF16), w_ref[...])
    gate = mod_ref[0][:, 2 * D_MODEL:]
    o_ref[0] = x_ref[0] + gate * out


def _odd_out(proj, x, mods, mod_base, mod_stride, conv_w, conv_b, w_out, tm):
    bsz, t_len, _ = x.shape
    per = tm // HALO
    n_halo = t_len // HALO
    main = lambda col: pl.BlockSpec((1, tm, C_WIDTH), lambda b, i: (b, i, col))
    prev = lambda col: pl.BlockSpec((1, HALO, C_WIDTH), lambda b, i: (b, jnp.maximum(i * per - 1, 0), col))
    nxt = lambda col: pl.BlockSpec((1, HALO, C_WIDTH),
                                   lambda b, i: (b, jnp.minimum((i + 1) * per, n_halo - 1), col))
    xspec = pl.BlockSpec((1, tm, D_MODEL), lambda b, i: (b, i, 0))
    return pl.pallas_call(
        _odd_out_kernel,
        grid=(bsz, t_len // tm),
        in_specs=[main(0), main(1), main(2), main(3), prev(1), prev(2), nxt(1), nxt(2),
                  pl.BlockSpec(conv_w.shape, lambda b, i: (0, 0)),
                  pl.BlockSpec((1, C_WIDTH), lambda b, i: (0, 0)),
                  xspec,
                  pl.BlockSpec((1, 1, 3 * D_MODEL), lambda b, i: (mod_base + mod_stride * b, 0, 0)),
                  pl.BlockSpec((C_WIDTH, D_MODEL), lambda b, i: (0, 0))],
        out_specs=xspec,
        out_shape=jax.ShapeDtypeStruct(x.shape, F32),
        compiler_params=_params("parallel", "parallel"),
        name="odd_out",
    )(proj, proj, proj, proj, proj, proj, proj, proj, conv_w, conv_b, x, mods, w_out)


def _rope_tables(t_len):
    pos = jnp.arange(t_len, dtype=jnp.int32)
    row = (pos // GRID_W).astype(F32)
    col = (pos % GRID_W).astype(F32)
    half = B_HD // 2
    inv = ROPE_BASE ** (-jnp.arange(0, half, 2, dtype=F32) / half)
    ar = row[:, None] * inv
    ac = col[:, None] * inv
    cos = jnp.concatenate([jnp.cos(ar), jnp.cos(ar), jnp.cos(ac), jnp.cos(ac)], axis=-1)
    sin = jnp.concatenate([-jnp.sin(ar), jnp.sin(ar), -jnp.sin(ac), jnp.sin(ac)], axis=-1)
    return jnp.tile(cos, (1, 2)), jnp.tile(sin, (1, 2))


def _proj_rows(x, norm_w, mods, mod_base, mod_stride, w, flatten):
    bsz, t_len, _ = x.shape
    if flatten:
        x = x.reshape(1, bsz * t_len, D_MODEL)
    rows = x.shape[1]
    tm = next(c for c in (1024, 512, 256, 128) if rows % c == 0)
    out = _in_proj(x, norm_w, mods, mod_base, mod_stride, w, tm)
    return out.reshape(bsz, t_len, -1)


def _even_layer(x, mods, mod_base, mod_stride, flatten, norm_w, w_in, w_out, rw, dp, lam_init,
                s0, ctx_k, ctx_v, rope, emit_cache):
    bsz, t_len, _ = x.shape
    proj = _proj_rows(x, norm_w, mods, mod_base, mod_stride, w_in, flatten)
    mu_rkv, mu_lo, w0, w_up, a0, a_up, k_k, k_a, r_k, lnx_w, lnx_b = rw
    q_norm2, k_norm2, lam_vec, subln = dp
    tm = min(t_len, 256)
    cos, sin = rope if rope is not None else (None, None)
    prep = _qk_prep(proj, q_norm2, k_norm2, cos, sin, tm, emit_cache)
    qn, kn, vn = prep[:3]
    yb = _attention(lam_vec, qn, kn, vn, ctx_k, ctx_v, proj, subln, lam_init, tm)
    o, bon, s_new = _rwkv(proj, s0, mu_rkv, mu_lo, w0, w_up, a0, a_up, k_k, k_a, r_k)
    x_new = _even_out(o, bon, proj, yb, x, mods, mod_base, mod_stride, lnx_w, lnx_b, w_out, tm)
    k_cache = prep[3] if emit_cache else None
    v_cache = proj[:, :, 6 * D_MODEL:7 * D_MODEL] if emit_cache else None
    return x_new, s_new, k_cache, v_cache


def _odd_layer(x, mods, mod_base, mod_stride, flatten, norm_w, w_in, conv_w, conv_b, w_out):
    proj = _proj_rows(x, norm_w, mods, mod_base, mod_stride, w_in, flatten)
    tm = min(x.shape[1], 256)
    return _odd_out(proj, x, mods, mod_base, mod_stride, conv_w, conv_b, w_out, tm)


def kernel(x_prompt, x_sample, state_rwkv_fwd, state_rwkv_bwd, cache_diff_k, cache_diff_v, c, c_ctx,
           norm_w, ada_w, ada_b,
           e_w_in, e_w_out, e_mu, e_w0, e_w_up, e_a0, e_a_up, e_k_k, e_k_a, e_r_k, e_lnx_w, e_lnx_b,
           e_q_norm, e_k_norm, e_lambda, e_subln,
           o_w_in, o_conv_w, o_conv_b, o_w_out):
    depth = norm_w.shape[0]
    bsz, seq = x_prompt.shape[:2]
    dec_b, dec_t = x_sample.shape[:2]
    past = cache_diff_k.shape[2]
    assert dec_b + 1 <= MOD_ROWS
    cvec = jnp.concatenate([c_ctx[None, :], c, jnp.zeros((MOD_ROWS - 1 - dec_b, D_MODEL), F32)], axis=0)
    mods = _ada(cvec, ada_w, ada_b)
    rope = _rope_tables(dec_t)
    rkv_w = 3 * A_WIDTH
    lora_end = rkv_w + 4 * LORA

    xp, xs = x_prompt, x_sample
    new_sf, new_sb, new_k, new_v = [], [], [], []
    for layer in range(depth):
        i = layer // 2
        nw = norm_w[layer][None, :]
        base = layer * MOD_ROWS
        if layer % 2 == 0:
            lam_init = 0.8 - 0.6 * math.exp(-0.3 * layer)
            w_in = jnp.concatenate([e_w_in[i][:, :rkv_w], e_w_in[i][:, lora_end:], e_w_in[i][:, rkv_w:lora_end]],
                                   axis=1).astype(BF16)
            w_out = e_w_out[i].astype(BF16)
            row = lambda t: t.reshape(1, -1)
            rw = (row(e_mu[i][:rkv_w]), row(e_mu[i][rkv_w:lora_end]), e_w0[i], e_w_up[i], e_a0[i], e_a_up[i],
                  row(e_k_k[i]), row(e_k_a[i]), row(e_r_k[i]), row(e_lnx_w[i]), row(e_lnx_b[i]))
            dp = (jnp.tile(e_q_norm[i], 2)[None, :], jnp.tile(e_k_norm[i], 2)[None, :], e_lambda[i],
                  row(e_subln[i]))
            zeros = jnp.zeros((2, bsz, A_HEADS, A_HEAD, A_HEAD), F32)
            xp, s_p, k_c, v_c = _even_layer(xp, mods, base, 0, True, nw, w_in, w_out, rw, dp, lam_init,
                                            zeros, None, None, None, True)
            s0 = jnp.stack([state_rwkv_fwd[:, i], state_rwkv_bwd[:, i]], axis=0)
            ctx_k = cache_diff_k[:, i].reshape(dec_b, past, B_WIDTH).astype(BF16)
            ctx_v = cache_diff_v[:, i].reshape(dec_b, past, B_WIDTH).astype(BF16)
            xs, _, _, _ = _even_layer(xs, mods, base + 1, 1, False, nw, w_in, w_out, rw, dp, lam_init,
                                      s0, ctx_k, ctx_v, rope, False)
            new_sf.append(s_p[0])
            new_sb.append(s_p[1])
            new_k.append(k_c.reshape(bsz, seq, B_HEADS, 2, B_HD))
            new_v.append(v_c.reshape(bsz, seq, B_HEADS, 2 * B_HD))
        else:
            w_in = o_w_in[i].astype(BF16)
            w_out = o_w_out[i].astype(BF16)
            cb = o_conv_b[i][None, :]
            xp = _odd_layer(xp, mods, base, 0, True, nw, w_in, o_conv_w[i], cb, w_out)
            xs = _odd_layer(xs, mods, base + 1, 1, False, nw, w_in, o_conv_w[i], cb, w_out)
    return (xp, xs, jnp.stack(new_sf, axis=1), jnp.stack(new_sb, axis=1),
            jnp.stack(new_k, axis=1), jnp.stack(new_v, axis=1))
```

```python
import functools
import math

import jax
import jax.numpy as jnp
from jax import lax
from jax.experimental import pallas as pl
from jax.experimental.pallas import tpu as pltpu

F32 = jnp.float32
BF16 = jnp.bfloat16
HI = lax.Precision.HIGHEST
PH = None

D_MODEL = 1024
A_WIDTH = 1024
A_HEAD = 64
A_HEADS = A_WIDTH // A_HEAD
LORA = 64
B_HD = 64
B_HEADS = 8
B_WIDTH = 1024
C_WIDTH = 2048
GRID_W = 64
ROPE_BASE = 10000.0
NORM_EPS = 1e-6
GN_EPS = 64e-5
MOD_ROWS = 8
CHUNK = 64
HALO = 8
VMEM_LIMIT = 48 * 1024 * 1024
ATTN_ROWS = 128

_NT = (((1,), (1,)), ((), ()))
_TN = (((0,), (0,)), ((), ()))


def _dot(x, y, dims=None, precision=None):
    if dims is None:
        return jnp.dot(x, y, precision=precision, preferred_element_type=F32)
    return lax.dot_general(x, y, dims, precision=precision, preferred_element_type=F32)


def _params(*sem):
    return pltpu.CompilerParams(dimension_semantics=sem, vmem_limit_bytes=VMEM_LIMIT)


def _ada_kernel(c_ref, w_ref, b_ref, o_ref):
    cv = c_ref[...]
    s = cv * jax.nn.sigmoid(cv)
    o_ref[0] = _dot(s, w_ref[0], precision=HI) + b_ref[0]


def _ada(cvec, ada_w, ada_b):
    depth = ada_w.shape[0]
    tn = D_MODEL
    out = pl.pallas_call(
        _ada_kernel,
        grid=(depth, 3 * D_MODEL // tn),
        in_specs=[pl.BlockSpec((MOD_ROWS, D_MODEL), lambda l, j: (0, 0)),
                  pl.BlockSpec((1, D_MODEL, tn), lambda l, j: (l, 0, j)),
                  pl.BlockSpec((1, 1, tn), lambda l, j: (l, 0, j))],
        out_specs=pl.BlockSpec((1, MOD_ROWS, tn), lambda l, j: (l, 0, j)),
        out_shape=jax.ShapeDtypeStruct((depth, MOD_ROWS, 3 * D_MODEL), F32),
        compiler_params=_params("parallel", "parallel"),
        name="ada",
    )(cvec, ada_w, ada_b.reshape(depth, 1, 3 * D_MODEL))
    return out.reshape(depth * MOD_ROWS, 1, 3 * D_MODEL)


def _in_proj_kernel(x_ref, nw_ref, mod_ref, w_ref, o_ref, h_ref):
    @pl.when(pl.program_id(2) == 0)
    def _():
        x = x_ref[0]
        ms = jnp.mean(x * x, axis=-1, keepdims=True)
        y = x * lax.rsqrt(ms + NORM_EPS) * nw_ref[...]
        m = mod_ref[0]
        h = y * (1.0 + m[:, D_MODEL:2 * D_MODEL]) + m[:, :D_MODEL]
        h_ref[...] = h.astype(BF16)

    o_ref[0] = _dot(h_ref[...], w_ref[...])


def _in_proj(x, norm_w, mods, mod_base, mod_stride, w, tm):
    bsz, t_len, _ = x.shape
    n_out = w.shape[1]
    tn = next(c for c in (1024, 768, 512, 256, 128) if n_out % c == 0)
    return pl.pallas_call(
        _in_proj_kernel,
        grid=(bsz, t_len // tm, n_out // tn),
        in_specs=[pl.BlockSpec((1, tm, D_MODEL), lambda b, i, j: (b, i, 0)),
                  pl.BlockSpec((1, D_MODEL), lambda b, i, j: (0, 0)),
                  pl.BlockSpec((1, 1, 3 * D_MODEL), lambda b, i, j: (mod_base + mod_stride * b, 0, 0)),
                  pl.BlockSpec((D_MODEL, tn), lambda b, i, j: (0, j))],
        out_specs=pl.BlockSpec((1, tm, tn), lambda b, i, j: (b, i, j)),
        out_shape=jax.ShapeDtypeStruct((bsz, t_len, n_out), F32),
        scratch_shapes=[pltpu.VMEM((tm, D_MODEL), BF16)],
        compiler_params=_params("parallel", "parallel", "arbitrary"),
        name="in_proj",
    )(x, norm_w, mods, w)


def _qk_prep_kernel(*refs, rope, emit_cache):
    pq_ref, pk_ref, pv_ref, qn_ref, kn_ref = refs[:5]
    pos = 5
    if rope:
        cos_ref, sin_ref = refs[pos:pos + 2]
        pos += 2
    q_out, k_out, v_out = refs[pos:pos + 3]
    kc_out = refs[pos + 3] if emit_cache else None
    width = 2 * B_HD
    tm = pq_ref.shape[1]
    lane = lax.broadcasted_iota(jnp.int32, (tm, width), 1)
    low = lane < B_HD
    first16 = (lane % 32) < 16

    def norm(x, w):
        ss = x * x
        s0 = jnp.sum(jnp.where(low, ss, 0.0), axis=-1, keepdims=True)
        s1 = jnp.sum(jnp.where(low, 0.0, ss), axis=-1, keepdims=True)
        ms = jnp.where(low, s0, s1) * (1.0 / B_HD)
        return x * lax.rsqrt(ms + NORM_EPS) * w

    def rot(x):
        if not rope:
            return x
        partner = jnp.where(first16, pltpu.roll(x, width - 16, 1), pltpu.roll(x, 16, 1))
        return x * cos_ref[...] + partner * sin_ref[...]

    for h in range(B_HEADS):
        sl = slice(h * width, (h + 1) * width)
        q = norm(pq_ref[0, :, sl], qn_ref[...])
        k = norm(pk_ref[0, :, sl], kn_ref[...])
        if emit_cache:
            kc_out[0, :, sl] = k
        q_out[0, :, sl] = (rot(q) * (B_HD ** -0.5 * math.log2(math.e))).astype(BF16)
        k_out[0, :, sl] = rot(k).astype(BF16)
    v_out[0] = pv_ref[0].astype(BF16)


def _qk_prep(proj, q_norm2, k_norm2, cos, sin, tm, emit_cache):
    bsz, t_len, _ = proj.shape
    rope = cos is not None
    col = lambda c: pl.BlockSpec((1, tm, B_WIDTH), lambda b, i: (b, i, c))
    vec = pl.BlockSpec((1, 2 * B_HD), lambda b, i: (0, 0))
    in_specs = [col(4), col(5), col(6), vec, vec]
    args = [proj, proj, proj, q_norm2, k_norm2]
    if rope:
        tab = pl.BlockSpec((tm, 2 * B_HD), lambda b, i: (i, 0))
        in_specs += [tab, tab]
        args += [cos, sin]
    out_spec = pl.BlockSpec((1, tm, B_WIDTH), lambda b, i: (b, i, 0))
    out_shape = [jax.ShapeDtypeStruct((bsz, t_len, B_WIDTH), BF16)] * 3
    out_specs = [out_spec] * 3
    if emit_cache:
        out_shape = out_shape + [jax.ShapeDtypeStruct((bsz, t_len, B_WIDTH), F32)]
        out_specs = out_specs + [out_spec]
    return pl.pallas_call(
        functools.partial(_qk_prep_kernel, rope=rope, emit_cache=emit_cache),
        grid=(bsz, t_len // tm),
        in_specs=in_specs, out_specs=out_specs, out_shape=out_shape,
        compiler_params=_params("parallel", "parallel"),
        name="qk_prep",
    )(*args)


def _attn_kernel(*refs, lam_init, has_ctx):
    lam_ref, q_ref, k_ref, v_ref = refs[:4]
    pos = 4
    if has_ctx:
        ck_ref, cv_ref = refs[pos:pos + 2]
        pos += 2
    gb_ref, sub_ref, o_ref, k_s, v_s = refs[pos:pos + 5]
    width = 2 * B_HD
    t_len = k_ref.shape[1]
    tq = q_ref.shape[1]

    @pl.when(pl.program_id(2) == 0)
    def _():
        k_s[0:t_len, :] = k_ref[0]
        v_s[0:t_len, 0:width] = v_ref[0]
        if has_ctx:
            k_s[t_len:, :] = ck_ref[0]
            v_s[t_len:, 0:width] = cv_ref[0]
        v_s[:, width:] = jnp.ones((v_s.shape[0], width), BF16)

    q = q_ref[0]
    lane = lax.broadcasted_iota(jnp.int32, q.shape, 1)
    zero = jnp.zeros_like(q)
    q2 = jnp.concatenate([jnp.where(lane < B_HD, q, zero), jnp.where(lane < B_HD, zero, q)], axis=0)
    n_groups = 2 * tq // ATTN_ROWS
    scores = lambda g: _dot(q2[g * ATTN_ROWS:(g + 1) * ATTN_ROWS], k_s[...], _NT)

    def softmax_pv(s):
        m = jnp.max(s, axis=-1, keepdims=True)
        acc = _dot(jnp.exp2(s - m).astype(BF16), v_s[...])
        return acc[:, :width] / acc[:, width:]

    parts = []
    s_next = scores(0)
    for g in range(n_groups):
        s_cur = s_next
        if g + 1 < n_groups:
            s_next = scores(g + 1)
        parts.append(softmax_pv(s_cur))
    comp = jnp.concatenate(parts, axis=0)

    lv = lam_ref[...]
    lam = (jnp.exp(jnp.sum(lv[0:1] * lv[1:2], axis=-1, keepdims=True))
           - jnp.exp(jnp.sum(lv[2:3] * lv[3:4], axis=-1, keepdims=True)) + lam_init)
    o = comp[:tq] - lam * comp[tq:]
    ms = jnp.mean(o * o, axis=-1, keepdims=True)
    o = o * lax.rsqrt(ms + NORM_EPS) * sub_ref[...] * (1.0 - lam_init)
    gb = gb_ref[0]
    o_ref[0] = (o * (gb * jax.nn.sigmoid(gb))).astype(BF16)


def _attention(lam_vec, q, k, v, ctx_k, ctx_v, proj, subln, lam_init, tq):
    bsz, t_len, _ = q.shape
    has_ctx = ctx_k is not None
    width = 2 * B_HD
    qspec = pl.BlockSpec((1, tq, width), lambda b, h, i: (b, i, h))
    kspec = pl.BlockSpec((1, t_len, width), lambda b, h, i: (b, 0, h))
    in_specs = [pl.BlockSpec((4, B_HD), lambda b, h, i: (0, 0)), qspec, kspec, kspec]
    args = [lam_vec, q, k, v]
    if has_ctx:
        cspec = pl.BlockSpec((1, ctx_k.shape[1], width), lambda b, h, i: (b, 0, h))
        in_specs += [cspec, cspec]
        args += [ctx_k, ctx_v]
    in_specs += [pl.BlockSpec((1, tq, width), lambda b, h, i: (b, i, 7 * B_HEADS + h)),
                 pl.BlockSpec((1, width), lambda b, h, i: (0, 0))]
    args += [proj, subln]
    n_keys = t_len + (ctx_k.shape[1] if has_ctx else 0)
    return pl.pallas_call(
        functools.partial(_attn_kernel, lam_init=lam_init, has_ctx=has_ctx),
        grid=(bsz, B_HEADS, t_len // tq),
        in_specs=in_specs, out_specs=qspec,
        out_shape=jax.ShapeDtypeStruct((bsz, t_len, B_WIDTH), BF16),
        scratch_shapes=[pltpu.VMEM((n_keys, width), BF16), pltpu.VMEM((n_keys, 2 * width), BF16)],
        compiler_params=_params("parallel", "parallel", "arbitrary"),
        name="diff_attn",
    )(*args)


def _rwkv_kernel(rkv_ref, rkv_p_ref, rkv_n_ref, lo_ref, lo_p_ref, lo_n_ref, mu_rkv_ref, mu_lo_ref,
                 w0_ref, wup_ref, a0_ref, aup_ref, kk_ref, ka_ref, rk_ref, s0_ref,
                 o_ref, bon_ref, s_ref,
                 kk_s, pex_s, apinv_s, kp_s, r_s, v_s, rkd_s, ptot_s, st_s, *, n_chunks):
    C = CHUNK
    d = pl.program_id(1)
    c = pl.program_id(2)
    cc = c + d * (n_chunks - 1 - 2 * c)
    first = cc == 0
    last = cc == n_chunks - 1
    row = lax.broadcasted_iota(jnp.int32, (C, 1), 0)

    def token_shift(x, x_prev, x_next, mu):
        prev_row = jnp.where(first, 0.0, x_prev[HALO - 1:HALO, :])
        next_row = jnp.where(last, 0.0, x_next[0:1, :])
        xp = jnp.where(row == 0, prev_row, pltpu.roll(x, 1, 0))
        xn = jnp.where(row == C - 1, next_row, pltpu.roll(x, C - 1, 0))
        return x + mu * (0.5 * (xp + xn) - x)

    p = token_shift(rkv_ref[0], rkv_p_ref[0], rkv_n_ref[0], mu_rkv_ref[...])
    lo = token_shift(lo_ref[0], lo_p_ref[0], lo_n_ref[0], mu_lo_ref[...])
    r = p[:, :A_WIDTH]
    k = p[:, A_WIDTH:2 * A_WIDTH]
    v = p[:, 2 * A_WIDTH:]
    fwd = d == 0
    wl = jnp.where(fwd, lo[:, 0:LORA], lo[:, LORA:2 * LORA])
    al = jnp.where(fwd, lo[:, 2 * LORA:3 * LORA], lo[:, 3 * LORA:])
    xw = w0_ref[0] + _dot(jnp.tanh(wl), wup_ref[0], precision=HI)
    logw = -math.exp(-0.5) * jax.nn.sigmoid(xw)
    a = jax.nn.sigmoid(a0_ref[0] + _dot(al, aup_ref[0], precision=HI))
    kd = k * (1.0 + (a - 1.0) * ka_ref[...])

    ti = lax.broadcasted_iota(jnp.int32, (C, C), 0)
    si = lax.broadcasted_iota(jnp.int32, (C, C), 1)
    before = (ti - si) * (1 - 2 * d)
    strict = before > 0
    incl = before >= 0
    cum = _dot(jnp.where(incl, 1.0, 0.0), logw, precision=HI)
    kk_s[...] = k * kk_ref[...]
    pex_s[...] = jnp.exp(cum - logw)
    pinv = jnp.exp(-cum)
    apinv_s[...] = a * pinv
    kp_s[...] = kd * pinv
    r_s[...] = r * jnp.exp(cum)
    v_s[...] = v
    rkd_s[...] = r * kd * rk_ref[...]
    ptot_s[...] = jnp.broadcast_to(jnp.exp(jnp.sum(logw, axis=0, keepdims=True)), (HALO, A_WIDTH))

    @pl.when(c == 0)
    def _():
        st_s[...] = s0_ref[0, 0]

    W2 = 2 * A_HEAD
    lane = lax.broadcasted_iota(jnp.int32, (C, W2), 1)
    trow = lax.broadcasted_iota(jnp.int32, (C, W2), 0)
    low = lane < A_HEAD
    before2 = (trow - lane % A_HEAD) * (1 - 2 * d)
    strict2 = before2 > 0
    incl2 = before2 >= 0
    eye2 = jnp.where(before2 == 0, 1.0, 0.0)
    brow = lax.broadcasted_iota(jnp.int32, (W2, W2), 0)
    bcol = lax.broadcasted_iota(jnp.int32, (W2, W2), 1)
    bdmask = (brow < A_HEAD) == (bcol < A_HEAD)
    bdmask2 = jnp.concatenate([bdmask, bdmask], axis=1)
    eye_bd = jnp.where(brow == bcol, 1.0, 0.0)

    def bd(x):
        return jnp.where(bdmask if x.shape[1] == W2 else bdmask2, jnp.concatenate([x, x], axis=0), 0.0)

    def head_sum(x):
        s0 = jnp.sum(jnp.where(low, x, 0.0), axis=-1, keepdims=True)
        s1 = jnp.sum(jnp.where(low, 0.0, x), axis=-1, keepdims=True)
        return jnp.where(low, s0, s1)

    def each(fn, *lists):
        return [fn(*xs) for xs in zip(*lists)]

    def rows(*xs):
        return jnp.concatenate(xs, axis=0)

    ps = range(A_HEADS // 2)
    sls = [slice(p * W2, (p + 1) * W2) for p in ps]
    kk = [kk_s[:, sl] for sl in sls]
    kk = [x / jnp.maximum(jnp.sqrt(head_sum(x * x)), 1e-12) for x in kk]
    pt = [ptot_s[0:1, sl] for sl in sls]
    am = [kk[p] * pex_s[:, sls[p]] for p in ps]
    bm = [kk[p] * apinv_s[:, sls[p]] for p in ps]
    kp = [kp_s[:, sl] for sl in sls]
    rm = [r_s[:, sl] for sl in sls]
    vm = [v_s[:, sl] for sl in sls]
    ar = each(rows, am, rm)
    xb = each(lambda x, y: _dot(x, bd(y), _NT, PH), ar, bm)
    xk = each(lambda x, y: _dot(x, bd(y), _NT, PH), ar, kp)
    nab = [jnp.where(strict2, -x[:C], 0.0) for x in xb]
    mak = [jnp.where(strict2, x[:C], 0.0) for x in xk]
    mrb = [jnp.where(incl2, x[C:], 0.0) for x in xb]
    mrk = [jnp.where(incl2, x[C:], 0.0) for x in xk]
    tinv = [eye2 + n for n in nab]
    pw = each(lambda n: _dot(n, bd(n), precision=PH), nab)
    for _ in range(int(math.log2(C)) - 2):
        both = each(lambda t, q: _dot(rows(t, q), bd(q), precision=PH), tinv, pw)
        tinv = each(lambda t, b: t + b[:C], tinv, both)
        pw = [b[C:] for b in both]
    tinv = each(lambda t, q: t + _dot(t, bd(q), precision=PH), tinv, pw)
    wm = each(lambda t, a: _dot(t, bd(a), precision=PH), tinv, am)
    mv = each(lambda ma, mr, v: _dot(rows(ma, mr), bd(v), precision=PH), mak, mrk, vm)
    u0 = each(lambda t, x: _dot(t, bd(x[:C]), precision=PH), tinv, mv)
    wu = each(lambda w, u: jnp.concatenate([w, u], axis=1), wm, u0)
    mw = each(lambda m, x: _dot(m, bd(x), precision=PH), mrb, wu)
    rp = each(lambda r, x: r - x[:, :W2], rm, mw)
    o0 = each(lambda x, y: x[C:] - y[:, W2:], mv, mw)
    bwu = each(lambda b, p, x: _dot(b * p, x, _TN, PH), bm, pt, wu)
    kv = each(lambda k, p, v: _dot(k * p, v, _TN, PH), kp, pt, vm)
    gt = each(lambda p, x: eye_bd * p - jnp.where(bdmask, x[:, :W2], 0.0), pt, bwu)
    ht = each(lambda x, y: jnp.where(bdmask, x - y[:, W2:], 0.0), kv, bwu)
    for p in ps:
        oc = _dot(rows(rp[p], gt[p]), st_s[p], precision=PH)
        o_ref[0, 0, :, sls[p]] = oc[:C] + o0[p]
        st_s[p] = oc[C:] + ht[p]
        bon_ref[0, 0, :, sls[p]] = head_sum(rkd_s[:, sls[p]]) * vm[p]

    @pl.when(c == n_chunks - 1)
    def _():
        s_ref[0, 0] = st_s[...]


_STATE_BD = (A_HEADS // 2, 2 * A_HEAD, 2 * A_HEAD)


def _state_to_bd(s):
    lead = s.shape[:-3]
    st = jnp.swapaxes(s, -1, -2).reshape(lead + (A_HEADS // 2, 2, A_HEAD, A_HEAD))
    z = jnp.zeros_like(st[..., 0, :, :])
    top = jnp.concatenate([st[..., 0, :, :], z], axis=-1)
    bot = jnp.concatenate([z, st[..., 1, :, :]], axis=-1)
    return jnp.concatenate([top, bot], axis=-2)


def _state_from_bd(st):
    pair = jnp.stack([st[..., :A_HEAD, :A_HEAD], st[..., A_HEAD:, A_HEAD:]], axis=-3)
    return jnp.swapaxes(pair.reshape(st.shape[:-3] + (A_HEADS, A_HEAD, A_HEAD)), -1, -2)


def _rwkv(proj, s0, mu_rkv, mu_lo, w0, w_up, a0, a_up, k_k, k_a, r_k):
    bsz, t_len, _ = proj.shape
    C = CHUNK
    n_chunks = t_len // C
    n_halo = t_len // HALO
    per = C // HALO
    chunk = lambda d, c: c + d * (n_chunks - 1 - 2 * c)
    rkv_w = 3 * A_WIDTH
    lo_w = 4 * LORA
    lo_col = (8 * D_MODEL) // lo_w

    def main(width, col):
        return pl.BlockSpec((1, C, width), lambda b, d, c: (b, chunk(d, c), col))

    def prev(width, col):
        return pl.BlockSpec((1, HALO, width), lambda b, d, c: (b, jnp.maximum(chunk(d, c) * per - 1, 0), col))

    def nxt(width, col):
        return pl.BlockSpec((1, HALO, width),
                            lambda b, d, c: (b, jnp.minimum((chunk(d, c) + 1) * per, n_halo - 1), col))

    vec = lambda width: pl.BlockSpec((1, width), lambda b, d, c: (0, 0))
    dvec = pl.BlockSpec((1, 1, A_WIDTH), lambda b, d, c: (d, 0, 0))
    dmat = pl.BlockSpec((1, LORA, A_WIDTH), lambda b, d, c: (d, 0, 0))
    sspec = pl.BlockSpec((1, 1) + _STATE_BD, lambda b, d, c: (d, b, 0, 0, 0))
    ospec = pl.BlockSpec((1, 1, C, A_WIDTH), lambda b, d, c: (d, b, chunk(d, c), 0))
    big = jax.ShapeDtypeStruct((2, bsz, t_len, A_WIDTH), F32)
    return pl.pallas_call(
        functools.partial(_rwkv_kernel, n_chunks=n_chunks),
        grid=(bsz, 2, n_chunks),
        in_specs=[main(rkv_w, 0), prev(rkv_w, 0), nxt(rkv_w, 0),
                  main(lo_w, lo_col), prev(lo_w, lo_col), nxt(lo_w, lo_col),
                  vec(rkv_w), vec(lo_w), dvec, dmat, dvec, dmat, vec(A_WIDTH), vec(A_WIDTH), vec(A_WIDTH), sspec],
        out_specs=[ospec, ospec, sspec],
        out_shape=[big, big, jax.ShapeDtypeStruct((2, bsz) + _STATE_BD, F32)],
        scratch_shapes=([pltpu.VMEM((C, A_WIDTH), F32)] * 7
                        + [pltpu.VMEM((HALO, A_WIDTH), F32), pltpu.VMEM(_STATE_BD, F32)]),
        compiler_params=_params("parallel", "arbitrary", "arbitrary"),
        name="rwkv",
    )(proj, proj, proj, proj, proj, proj, mu_rkv, mu_lo,
      w0.reshape(2, 1, A_WIDTH), w_up, a0.reshape(2, 1, A_WIDTH), a_up, k_k, k_a, r_k, s0)


def _even_out_kernel(of_ref, ob_ref, bf_ref, bb_ref, ga_ref, yb_ref, x_ref, mod_ref, lw_ref, lb_ref, w_ref,
                     o_ref, y_s):
    o = of_ref[0, 0] + ob_ref[0, 0]
    bon = bf_ref[0, 0] + bb_ref[0, 0]
    ga = ga_ref[0]
    gate_a = ga * jax.nn.sigmoid(ga)
    for h in range(A_HEADS):
        sl = slice(h * A_HEAD, (h + 1) * A_HEAD)
        oh = o[:, sl]
        mean = jnp.mean(oh, axis=-1, keepdims=True)
        cen = oh - mean
        var = jnp.mean(cen * cen, axis=-1, keepdims=True)
        yh = cen * lax.rsqrt(var + GN_EPS) * lw_ref[:, sl] + lb_ref[:, sl]
        y_s[:, sl] = ((yh + bon[:, sl]) * gate_a[:, sl]).astype(BF16)
    y_s[:, A_WIDTH:] = yb_ref[0]
    out = _dot(y_s[...], w_ref[...])
    gate = mod_ref[0][:, 2 * D_MODEL:]
    o_ref[0] = x_ref[0] + gate * out


def _even_out(o, bon, proj, yb, x, mods, mod_base, mod_stride, lnx_w, lnx_b, w_out, tm):
    bsz, t_len, _ = x.shape
    dspec = lambda d: pl.BlockSpec((1, 1, tm, A_WIDTH), lambda b, i: (d, b, i, 0))
    row = lambda col: pl.BlockSpec((1, tm, D_MODEL), lambda b, i: (b, i, col))
    vec = pl.BlockSpec((1, A_WIDTH), lambda b, i: (0, 0))
    return pl.pallas_call(
        _even_out_kernel,
        grid=(bsz, t_len // tm),
        in_specs=[dspec(0), dspec(1), dspec(0), dspec(1), row(3), row(0), row(0),
                  pl.BlockSpec((1, 1, 3 * D_MODEL), lambda b, i: (mod_base + mod_stride * b, 0, 0)),
                  vec, vec, pl.BlockSpec((A_WIDTH + B_WIDTH, D_MODEL), lambda b, i: (0, 0))],
        out_specs=row(0),
        out_shape=jax.ShapeDtypeStruct(x.shape, F32),
        scratch_shapes=[pltpu.VMEM((tm, A_WIDTH + B_WIDTH), BF16)],
        compiler_params=_params("parallel", "parallel"),
        name="even_out",
    )(o, o, bon, bon, proj, yb, x, mods, lnx_w, lnx_b, w_out)


def _odd_out_kernel(bg_ref, cg_ref, u_ref, z_ref, cgp_ref, up_ref, cgn_ref, un_ref, cw_ref, cb_ref,
                    x_ref, mod_ref, w_ref, o_ref):
    i = pl.program_id(1)
    tm = bg_ref.shape[1]
    row = lax.broadcasted_iota(jnp.int32, (tm, 1), 0)
    cu = cg_ref[0] * u_ref[0]
    prev_row = jnp.where(i == 0, 0.0, cgp_ref[0][HALO - 1:HALO, :] * up_ref[0][HALO - 1:HALO, :])
    next_row = jnp.where(i == pl.num_programs(1) - 1, 0.0, cgn_ref[0][0:1, :] * un_ref[0][0:1, :])
    cu_prev = jnp.where(row == 0, prev_row, pltpu.roll(cu, 1, 0))
    cu_next = jnp.where(row == tm - 1, next_row, pltpu.roll(cu, tm - 1, 0))
    cw = cw_ref[...]
    conv = cu_prev * cw[0:1] + cu * cw[1:2] + cu_next * cw[2:3] + cb_ref[...]
    z = z_ref[0]
    y = bg_ref[0] * conv * (z * jax.nn.sigmoid(z))
    out = _dot(y.astype(BF16), w_ref[...])
    gate = mod_ref[0][:, 2 * D_MODEL:]
    o_ref[0] = x_ref[0] + gate * out


def _odd_out(proj, x, mods, mod_base, mod_stride, conv_w, conv_b, w_out, tm):
    bsz, t_len, _ = x.shape
    per = tm // HALO
    n_halo = t_len // HALO
    main = lambda col: pl.BlockSpec((1, tm, C_WIDTH), lambda b, i: (b, i, col))
    prev = lambda col: pl.BlockSpec((1, HALO, C_WIDTH), lambda b, i: (b, jnp.maximum(i * per - 1, 0), col))
    nxt = lambda col: pl.BlockSpec((1, HALO, C_WIDTH),
                                   lambda b, i: (b, jnp.minimum((i + 1) * per, n_halo - 1), col))
    xspec = pl.BlockSpec((1, tm, D_MODEL), lambda b, i: (b, i, 0))
    return pl.pallas_call(
        _odd_out_kernel,
        grid=(bsz, t_len // tm),
        in_specs=[main(0), main(1), main(2), main(3), prev(1), prev(2), nxt(1), nxt(2),
                  pl.BlockSpec(conv_w.shape, lambda b, i: (0, 0)),
                  pl.BlockSpec((1, C_WIDTH), lambda b, i: (0, 0)),
                  xspec,
                  pl.BlockSpec((1, 1, 3 * D_MODEL), lambda b, i: (mod_base + mod_stride * b, 0, 0)),
                  pl.BlockSpec((C_WIDTH, D_MODEL), lambda b, i: (0, 0))],
        out_specs=xspec,
        out_shape=jax.ShapeDtypeStruct(x.shape, F32),
        compiler_params=_params("parallel", "parallel"),
        name="odd_out",
    )(proj, proj, proj, proj, proj, proj, proj, proj, conv_w, conv_b, x, mods, w_out)


def _rope_tables(t_len):
    pos = jnp.arange(t_len, dtype=jnp.int32)
    row = (pos // GRID_W).astype(F32)
    col = (pos % GRID_W).astype(F32)
    half = B_HD // 2
    inv = ROPE_BASE ** (-jnp.arange(0, half, 2, dtype=F32) / half)
    ar = row[:, None] * inv
    ac = col[:, None] * inv
    cos = jnp.concatenate([jnp.cos(ar), jnp.cos(ar), jnp.cos(ac), jnp.cos(ac)], axis=-1)
    sin = jnp.concatenate([-jnp.sin(ar), jnp.sin(ar), -jnp.sin(ac), jnp.sin(ac)], axis=-1)
    return jnp.tile(cos, (1, 2)), jnp.tile(sin, (1, 2))


def _proj_rows(x, norm_w, mods, mod_base, mod_stride, w, flatten):
    bsz, t_len, _ = x.shape
    if flatten:
        x = x.reshape(1, bsz * t_len, D_MODEL)
    rows = x.shape[1]
    tm = next(c for c in (1024, 512, 256, 128) if rows % c == 0)
    out = _in_proj(x, norm_w, mods, mod_base, mod_stride, w, tm)
    return out.reshape(bsz, t_len, -1)


def _even_layer(x, mods, mod_base, mod_stride, flatten, norm_w, w_in, w_out, rw, dp, lam_init,
                s0, ctx_k, ctx_v, rope, emit_cache):
    bsz, t_len, _ = x.shape
    proj = _proj_rows(x, norm_w, mods, mod_base, mod_stride, w_in, flatten)
    mu_rkv, mu_lo, w0, w_up, a0, a_up, k_k, k_a, r_k, lnx_w, lnx_b = rw
    q_norm2, k_norm2, lam_vec, subln = dp
    tm = min(t_len, 256)
    cos, sin = rope if rope is not None else (None, None)
    prep = _qk_prep(proj, q_norm2, k_norm2, cos, sin, tm, emit_cache)
    qn, kn, vn = prep[:3]
    yb = _attention(lam_vec, qn, kn, vn, ctx_k, ctx_v, proj, subln, lam_init, min(t_len, 512))
    o, bon, s_new = _rwkv(proj, s0, mu_rkv, mu_lo, w0, w_up, a0, a_up, k_k, k_a, r_k)
    x_new = _even_out(o, bon, proj, yb, x, mods, mod_base, mod_stride, lnx_w, lnx_b, w_out, tm)
    k_cache = prep[3] if emit_cache else None
    v_cache = proj[:, :, 6 * D_MODEL:7 * D_MODEL] if emit_cache else None
    return x_new, s_new, k_cache, v_cache


def _odd_layer(x, mods, mod_base, mod_stride, flatten, norm_w, w_in, conv_w, conv_b, w_out):
    proj = _proj_rows(x, norm_w, mods, mod_base, mod_stride, w_in, flatten)
    tm = min(x.shape[1], 256)
    return _odd_out(proj, x, mods, mod_base, mod_stride, conv_w, conv_b, w_out, tm)


def kernel(x_prompt, x_sample, state_rwkv_fwd, state_rwkv_bwd, cache_diff_k, cache_diff_v, c, c_ctx,
           norm_w, ada_w, ada_b,
           e_w_in, e_w_out, e_mu, e_w0, e_w_up, e_a0, e_a_up, e_k_k, e_k_a, e_r_k, e_lnx_w, e_lnx_b,
           e_q_norm, e_k_norm, e_lambda, e_subln,
           o_w_in, o_conv_w, o_conv_b, o_w_out):
    depth = norm_w.shape[0]
    bsz, seq = x_prompt.shape[:2]
    dec_b, dec_t = x_sample.shape[:2]
    past = cache_diff_k.shape[2]
    assert dec_b + 1 <= MOD_ROWS
    cvec = jnp.concatenate([c_ctx[None, :], c, jnp.zeros((MOD_ROWS - 1 - dec_b, D_MODEL), F32)], axis=0)
    mods = _ada(cvec, ada_w, ada_b)
    rope = _rope_tables(dec_t)
    rkv_w = 3 * A_WIDTH
    lora_end = rkv_w + 4 * LORA

    xp, xs = x_prompt, x_sample
    new_sf, new_sb, new_k, new_v = [], [], [], []
    for layer in range(depth):
        i = layer // 2
        nw = norm_w[layer][None, :]
        base = layer * MOD_ROWS
        if layer % 2 == 0:
            lam_init = 0.8 - 0.6 * math.exp(-0.3 * layer)
            w_in = jnp.concatenate([e_w_in[i][:, :rkv_w], e_w_in[i][:, lora_end:], e_w_in[i][:, rkv_w:lora_end]],
                                   axis=1).astype(BF16)
            w_out = e_w_out[i].astype(BF16)
            row = lambda t: t.reshape(1, -1)
            rw = (row(e_mu[i][:rkv_w]), row(e_mu[i][rkv_w:lora_end]), e_w0[i], e_w_up[i], e_a0[i], e_a_up[i],
                  row(e_k_k[i]), row(e_k_a[i]), row(e_r_k[i]), row(e_lnx_w[i]), row(e_lnx_b[i]))
            dp = (jnp.tile(e_q_norm[i], 2)[None, :], jnp.tile(e_k_norm[i], 2)[None, :], e_lambda[i],
                  row(e_subln[i]))
            zeros = jnp.zeros((2, bsz) + _STATE_BD, F32)
            xp, s_p, k_c, v_c = _even_layer(xp, mods, base, 0, True, nw, w_in, w_out, rw, dp, lam_init,
                                            zeros, None, None, None, True)
            s_p = _state_from_bd(s_p)
            s0 = _state_to_bd(jnp.stack([state_rwkv_fwd[:, i], state_rwkv_bwd[:, i]], axis=0))
            ctx_k = cache_diff_k[:, i].reshape(dec_b, past, B_WIDTH).astype(BF16)
            ctx_v = cache_diff_v[:, i].reshape(dec_b, past, B_WIDTH).astype(BF16)
            xs, _, _, _ = _even_layer(xs, mods, base + 1, 1, False, nw, w_in, w_out, rw, dp, lam_init,
                                      s0, ctx_k, ctx_v, rope, False)
            new_sf.append(s_p[0])
            new_sb.append(s_p[1])
            new_k.append(k_c.reshape(bsz, seq, B_HEADS, 2, B_HD))
            new_v.append(v_c.reshape(bsz, seq, B_HEADS, 2 * B_HD))
        else:
            w_in = o_w_in[i].astype(BF16)
            w_out = o_w_out[i].astype(BF16)
            cb = o_conv_b[i][None, :]
            xp = _odd_layer(xp, mods, base, 0, True, nw, w_in, o_conv_w[i], cb, w_out)
            xs = _odd_layer(xs, mods, base + 1, 1, False, nw, w_in, o_conv_w[i], cb, w_out)
    return (xp, xs, jnp.stack(new_sf, axis=1), jnp.stack(new_sb, axis=1),
            jnp.stack(new_k, axis=1), jnp.stack(new_v, axis=1))
```

```python
import functools
import math

import jax
import jax.numpy as jnp
from jax import lax
from jax.experimental import pallas as pl
from jax.experimental.pallas import tpu as pltpu

F32 = jnp.float32
BF16 = jnp.bfloat16
HI = lax.Precision.HIGHEST
PH = None

D_MODEL = 1024
A_WIDTH = 1024
A_HEAD = 64
A_HEADS = A_WIDTH // A_HEAD
LORA = 64
B_HD = 64
B_HEADS = 8
B_WIDTH = 1024
C_WIDTH = 2048
GRID_W = 64
ROPE_BASE = 10000.0
NORM_EPS = 1e-6
GN_EPS = 64e-5
MOD_ROWS = 8
CHUNK = 64
RWKV_SUB = 2
SUBLANES = 8
HALO = 16
VMEM_LIMIT = 48 * 1024 * 1024
ATTN_ROWS = 128

_NT = (((1,), (1,)), ((), ()))
_TN = (((0,), (0,)), ((), ()))


def _dot(x, y, dims=None, precision=None):
    if dims is None:
        return jnp.dot(x, y, precision=precision, preferred_element_type=F32)
    return lax.dot_general(x, y, dims, precision=precision, preferred_element_type=F32)


def _params(*sem):
    return pltpu.CompilerParams(dimension_semantics=sem, vmem_limit_bytes=VMEM_LIMIT)


def _ada_kernel(c_ref, w_ref, b_ref, o_ref):
    cv = c_ref[...]
    s = cv * jax.nn.sigmoid(cv)
    o_ref[0] = _dot(s, w_ref[0], precision=HI) + b_ref[0]


def _ada(cvec, ada_w, ada_b):
    depth = ada_w.shape[0]
    tn = D_MODEL
    out = pl.pallas_call(
        _ada_kernel,
        grid=(depth, 3 * D_MODEL // tn),
        in_specs=[pl.BlockSpec((MOD_ROWS, D_MODEL), lambda l, j: (0, 0)),
                  pl.BlockSpec((1, D_MODEL, tn), lambda l, j: (l, 0, j)),
                  pl.BlockSpec((1, 1, tn), lambda l, j: (l, 0, j))],
        out_specs=pl.BlockSpec((1, MOD_ROWS, tn), lambda l, j: (l, 0, j)),
        out_shape=jax.ShapeDtypeStruct((depth, MOD_ROWS, 3 * D_MODEL), F32),
        compiler_params=_params("parallel", "parallel"),
        name="ada",
    )(cvec, ada_w, ada_b.reshape(depth, 1, 3 * D_MODEL))
    return out.reshape(depth * MOD_ROWS, 1, 3 * D_MODEL)


def _in_proj_kernel(x_ref, nw_ref, mod_ref, w_ref, o_ref, h_ref):
    @pl.when(pl.program_id(2) == 0)
    def _():
        x = x_ref[0]
        ms = jnp.mean(x * x, axis=-1, keepdims=True)
        y = x * lax.rsqrt(ms + NORM_EPS) * nw_ref[...]
        m = mod_ref[0]
        h = y * (1.0 + m[:, D_MODEL:2 * D_MODEL]) + m[:, :D_MODEL]
        h_ref[...] = h.astype(BF16)

    o_ref[0] = _dot(h_ref[...], w_ref[...]).astype(o_ref.dtype)


def _in_proj(x, norm_w, mods, mod_base, mod_stride, w, tm):
    bsz, t_len, _ = x.shape
    n_out = w.shape[1]
    tn = next(c for c in (1024, 768, 512, 256, 128) if n_out % c == 0)
    return pl.pallas_call(
        _in_proj_kernel,
        grid=(bsz, t_len // tm, n_out // tn),
        in_specs=[pl.BlockSpec((1, tm, D_MODEL), lambda b, i, j: (b, i, 0)),
                  pl.BlockSpec((1, D_MODEL), lambda b, i, j: (0, 0)),
                  pl.BlockSpec((1, 1, 3 * D_MODEL), lambda b, i, j: (mod_base + mod_stride * b, 0, 0)),
                  pl.BlockSpec((D_MODEL, tn), lambda b, i, j: (0, j))],
        out_specs=pl.BlockSpec((1, tm, tn), lambda b, i, j: (b, i, j)),
        out_shape=jax.ShapeDtypeStruct((bsz, t_len, n_out), BF16),
        scratch_shapes=[pltpu.VMEM((tm, D_MODEL), BF16)],
        compiler_params=_params("parallel", "parallel", "arbitrary"),
        name="in_proj",
    )(x, norm_w, mods, w)


def _qk_prep_kernel(*refs, rope, emit_cache):
    pq_ref, pk_ref, qn_ref, kn_ref = refs[:4]
    pos = 4
    if rope:
        cos_ref, sin_ref = refs[pos:pos + 2]
        pos += 2
    q_out, k_out = refs[pos:pos + 2]
    kc_out = refs[pos + 2] if emit_cache else None
    width = 2 * B_HD
    tm = pq_ref.shape[1]
    lane = lax.broadcasted_iota(jnp.int32, (tm, width), 1)
    low = lane < B_HD
    first16 = (lane % 32) < 16

    def norm(x, w):
        ss = x * x
        s0 = jnp.sum(jnp.where(low, ss, 0.0), axis=-1, keepdims=True)
        s1 = jnp.sum(jnp.where(low, 0.0, ss), axis=-1, keepdims=True)
        ms = jnp.where(low, s0, s1) * (1.0 / B_HD)
        return x * lax.rsqrt(ms + NORM_EPS) * w

    def rot(x):
        if not rope:
            return x
        partner = jnp.where(first16, pltpu.roll(x, width - 16, 1), pltpu.roll(x, 16, 1))
        return x * cos_ref[...] + partner * sin_ref[...]

    for h in range(B_HEADS):
        sl = slice(h * width, (h + 1) * width)
        q = norm(pq_ref[0, :, sl].astype(F32), qn_ref[...])
        k = norm(pk_ref[0, :, sl].astype(F32), kn_ref[...])
        if emit_cache:
            kc_out[0, :, sl] = k
        q_out[0, :, sl] = (rot(q) * (B_HD ** -0.5 * math.log2(math.e))).astype(BF16)
        k_out[0, :, sl] = rot(k).astype(BF16)


def _qk_prep(proj, q_norm2, k_norm2, cos, sin, tm, emit_cache):
    bsz, t_len, _ = proj.shape
    rope = cos is not None
    col = lambda c: pl.BlockSpec((1, tm, B_WIDTH), lambda b, i: (b, i, c))
    vec = pl.BlockSpec((1, 2 * B_HD), lambda b, i: (0, 0))
    in_specs = [col(4), col(5), vec, vec]
    args = [proj, proj, q_norm2, k_norm2]
    if rope:
        tab = pl.BlockSpec((tm, 2 * B_HD), lambda b, i: (i, 0))
        in_specs += [tab, tab]
        args += [cos, sin]
    out_spec = pl.BlockSpec((1, tm, B_WIDTH), lambda b, i: (b, i, 0))
    out_shape = [jax.ShapeDtypeStruct((bsz, t_len, B_WIDTH), BF16)] * 2
    out_specs = [out_spec] * 2
    if emit_cache:
        out_shape = out_shape + [jax.ShapeDtypeStruct((bsz, t_len, B_WIDTH), F32)]
        out_specs = out_specs + [out_spec]
    return pl.pallas_call(
        functools.partial(_qk_prep_kernel, rope=rope, emit_cache=emit_cache),
        grid=(bsz, t_len // tm),
        in_specs=in_specs, out_specs=out_specs, out_shape=out_shape,
        compiler_params=_params("parallel", "parallel"),
        name="qk_prep",
    )(*args)


def _attn_kernel(*refs, lam_init, has_ctx):
    lam_ref, q_ref, k_ref, v_ref = refs[:4]
    pos = 4
    if has_ctx:
        ck_ref, cv_ref = refs[pos:pos + 2]
        pos += 2
    gb_ref, sub_ref, o_ref, k_s, v_s = refs[pos:pos + 5]
    width = 2 * B_HD
    t_len = k_ref.shape[1]
    tq = q_ref.shape[1]

    @pl.when(pl.program_id(2) == 0)
    def _():
        k_s[0:t_len, :] = k_ref[0]
        v_s[0:t_len, 0:width] = v_ref[0]
        if has_ctx:
            k_s[t_len:, :] = ck_ref[0]
            v_s[t_len:, 0:width] = cv_ref[0]
        v_s[:, width:] = jnp.ones((v_s.shape[0], width), BF16)

    q = q_ref[0]
    lane = lax.broadcasted_iota(jnp.int32, q.shape, 1)
    zero = jnp.zeros_like(q)
    q2 = jnp.concatenate([jnp.where(lane < B_HD, q, zero), jnp.where(lane < B_HD, zero, q)], axis=0)
    n_groups = 2 * tq // ATTN_ROWS
    scores = lambda g: _dot(q2[g * ATTN_ROWS:(g + 1) * ATTN_ROWS], k_s[...], _NT)

    def softmax_pv(s):
        m = jnp.max(s, axis=-1, keepdims=True)
        acc = _dot(jnp.exp2(s - m).astype(BF16), v_s[...])
        return acc[:, :width] / acc[:, width:]

    parts = []
    s_next = scores(0)
    for g in range(n_groups):
        s_cur = s_next
        if g + 1 < n_groups:
            s_next = scores(g + 1)
        parts.append(softmax_pv(s_cur))
    comp = jnp.concatenate(parts, axis=0)

    lv = lam_ref[...]
    lam = (jnp.exp(jnp.sum(lv[0:1] * lv[1:2], axis=-1, keepdims=True))
           - jnp.exp(jnp.sum(lv[2:3] * lv[3:4], axis=-1, keepdims=True)) + lam_init)
    o = comp[:tq] - lam * comp[tq:]
    ms = jnp.mean(o * o, axis=-1, keepdims=True)
    o = o * lax.rsqrt(ms + NORM_EPS) * sub_ref[...] * (1.0 - lam_init)
    gb = gb_ref[0].astype(F32)
    o_ref[0] = (o * (gb * jax.nn.sigmoid(gb))).astype(BF16)


def _attention(lam_vec, q, k, ctx_k, ctx_v, proj, subln, lam_init, tq):
    bsz, t_len, _ = q.shape
    has_ctx = ctx_k is not None
    width = 2 * B_HD
    qspec = pl.BlockSpec((1, tq, width), lambda b, h, i: (b, i, h))
    kspec = pl.BlockSpec((1, t_len, width), lambda b, h, i: (b, 0, h))
    vspec = pl.BlockSpec((1, t_len, width), lambda b, h, i: (b, 0, 6 * B_HEADS + h))
    in_specs = [pl.BlockSpec((4, B_HD), lambda b, h, i: (0, 0)), qspec, kspec, vspec]
    args = [lam_vec, q, k, proj]
    if has_ctx:
        cspec = pl.BlockSpec((1, ctx_k.shape[1], width), lambda b, h, i: (b, 0, h))
        in_specs += [cspec, cspec]
        args += [ctx_k, ctx_v]
    in_specs += [pl.BlockSpec((1, tq, width), lambda b, h, i: (b, i, 7 * B_HEADS + h)),
                 pl.BlockSpec((1, width), lambda b, h, i: (0, 0))]
    args += [proj, subln]
    n_keys = t_len + (ctx_k.shape[1] if has_ctx else 0)
    return pl.pallas_call(
        functools.partial(_attn_kernel, lam_init=lam_init, has_ctx=has_ctx),
        grid=(bsz, B_HEADS, t_len // tq),
        in_specs=in_specs, out_specs=qspec,
        out_shape=jax.ShapeDtypeStruct((bsz, t_len, B_WIDTH), BF16),
        scratch_shapes=[pltpu.VMEM((n_keys, width), BF16), pltpu.VMEM((n_keys, 2 * width), BF16)],
        compiler_params=_params("parallel", "parallel", "arbitrary"),
        name="diff_attn",
    )(*args)


def _rwkv_kernel(rkv_ref, rkv_p_ref, rkv_n_ref, lo_ref, lo_p_ref, lo_n_ref, mu_rkv_ref, mu_lo_ref,
                 w0_ref, wup_ref, a0_ref, aup_ref, kk_ref, ka_ref, rk_ref, s0_ref,
                 o_ref, bon_ref, s_ref,
                 kk_s, pex_s, apinv_s, kp_s, r_s, v_s, rkd_s, ptot_s, st_s, *, n_steps, n_sub, reverse):
    C = CHUNK
    R = n_sub * C
    c = pl.program_id(1)
    blk = n_steps - 1 - c if reverse else c
    first = blk == 0
    last = blk == n_steps - 1
    row = lax.broadcasted_iota(jnp.int32, (R, 1), 0)

    def token_shift(x, x_prev, x_next, mu):
        prev_row = jnp.where(first, 0.0, x_prev[HALO - 1:HALO, :])
        next_row = jnp.where(last, 0.0, x_next[0:1, :])
        xp = jnp.where(row == 0, prev_row, pltpu.roll(x, 1, 0))
        xn = jnp.where(row == R - 1, next_row, pltpu.roll(x, R - 1, 0))
        return x + mu * (0.5 * (xp + xn) - x)

    up = lambda ref: ref[0].astype(F32)
    p = token_shift(up(rkv_ref), up(rkv_p_ref), up(rkv_n_ref), mu_rkv_ref[...])
    lo = token_shift(up(lo_ref), up(lo_p_ref), up(lo_n_ref), mu_lo_ref[...])
    r = p[:, :A_WIDTH]
    k = p[:, A_WIDTH:2 * A_WIDTH]
    v = p[:, 2 * A_WIDTH:]
    dcol = 1 if reverse else 0
    wl = lo[:, dcol * LORA:(dcol + 1) * LORA]
    al = lo[:, (2 + dcol) * LORA:(3 + dcol) * LORA]
    xw = w0_ref[...] + _dot(jnp.tanh(wl), wup_ref[...], precision=HI)
    logw = -math.exp(-0.5) * jax.nn.sigmoid(xw)
    a = jax.nn.sigmoid(a0_ref[...] + _dot(al, aup_ref[...], precision=HI))
    kd = k * (1.0 + (a - 1.0) * ka_ref[...])

    ti = lax.broadcasted_iota(jnp.int32, (C, C), 0)
    si = lax.broadcasted_iota(jnp.int32, (C, C), 1)
    incl = (si >= ti) if reverse else (si <= ti)
    kk_s[...] = k * kk_ref[...]
    v_s[...] = v
    rkd_s[...] = r * kd * rk_ref[...]
    for j in range(n_sub):
        rs = slice(j * C, (j + 1) * C)
        lw = logw[rs]
        cum = _dot(jnp.where(incl, 1.0, 0.0), lw, precision=HI)
        pex_s[rs] = jnp.exp(cum - lw)
        pinv = jnp.exp(-cum)
        apinv_s[rs] = a[rs] * pinv
        kp_s[rs] = kd[rs] * pinv
        r_s[rs] = r[rs] * jnp.exp(cum)
        ptot_s[j] = jnp.broadcast_to(jnp.exp(jnp.sum(lw, axis=0, keepdims=True)), (SUBLANES, A_WIDTH))

    @pl.when(c == 0)
    def _():
        st_s[...] = s0_ref[0]

    W2 = 2 * A_HEAD
    lane = lax.broadcasted_iota(jnp.int32, (C, W2), 1)
    trow = lax.broadcasted_iota(jnp.int32, (C, W2), 0)
    low = lane < A_HEAD
    before2 = (lane % A_HEAD - trow) if reverse else (trow - lane % A_HEAD)
    strict2 = before2 > 0
    incl2 = before2 >= 0
    eye2 = jnp.where(before2 == 0, 1.0, 0.0)
    brow = lax.broadcasted_iota(jnp.int32, (W2, W2), 0)
    bcol = lax.broadcasted_iota(jnp.int32, (W2, W2), 1)
    bdmask = (brow < A_HEAD) == (bcol < A_HEAD)
    bdmask2 = jnp.concatenate([bdmask, bdmask], axis=1)
    eye_bd = jnp.where(brow == bcol, 1.0, 0.0)

    def bd(x):
        return jnp.where(bdmask if x.shape[1] == W2 else bdmask2, jnp.concatenate([x, x], axis=0), 0.0)

    def head_sum(x):
        s0 = jnp.sum(jnp.where(low, x, 0.0), axis=-1, keepdims=True)
        s1 = jnp.sum(jnp.where(low, 0.0, x), axis=-1, keepdims=True)
        return jnp.where(low, s0, s1)

    def each(fn, *lists):
        return [fn(*xs) for xs in zip(*lists)]

    def rows(*xs):
        return jnp.concatenate(xs, axis=0)

    n_pairs = A_HEADS // 2
    items = [(slice(j * C, (j + 1) * C), slice(p * W2, (p + 1) * W2)) for j in range(n_sub) for p in range(n_pairs)]
    kk = [kk_s[rs, sl] for rs, sl in items]
    kk = [x / jnp.maximum(jnp.sqrt(head_sum(x * x)), 1e-12) for x in kk]
    pt = [ptot_s[j, 0:1, slice(p * W2, (p + 1) * W2)] for j in range(n_sub) for p in range(n_pairs)]
    am = [kk[i] * pex_s[rs, sl] for i, (rs, sl) in enumerate(items)]
    bm = [kk[i] * apinv_s[rs, sl] for i, (rs, sl) in enumerate(items)]
    kp = [kp_s[rs, sl] for rs, sl in items]
    rm = [r_s[rs, sl] for rs, sl in items]
    vm = [v_s[rs, sl] for rs, sl in items]
    ar = each(rows, am, rm)
    xb = each(lambda x, y: _dot(x, bd(y), _NT, PH), ar, bm)
    xk = each(lambda x, y: _dot(x, bd(y), _NT, PH), ar, kp)
    nab = [jnp.where(strict2, -x[:C], 0.0) for x in xb]
    mak = [jnp.where(strict2, x[:C], 0.0) for x in xk]
    mrb = [jnp.where(incl2, x[C:], 0.0) for x in xb]
    mrk = [jnp.where(incl2, x[C:], 0.0) for x in xk]
    tinv = [eye2 + n for n in nab]
    pw = each(lambda n: _dot(n, bd(n), precision=PH), nab)
    for _ in range(int(math.log2(C)) - 2):
        both = each(lambda t, q: _dot(rows(t, q), bd(q), precision=PH), tinv, pw)
        tinv = each(lambda t, b: t + b[:C], tinv, both)
        pw = [b[C:] for b in both]
    tinv = each(lambda t, q: t + _dot(t, bd(q), precision=PH), tinv, pw)
    wm = each(lambda t, a: _dot(t, bd(a), precision=PH), tinv, am)
    mv = each(lambda ma, mr, v: _dot(rows(ma, mr), bd(v), precision=PH), mak, mrk, vm)
    u0 = each(lambda t, x: _dot(t, bd(x[:C]), precision=PH), tinv, mv)
    wu = each(lambda w, u: jnp.concatenate([w, u], axis=1), wm, u0)
    mw = each(lambda m, x: _dot(m, bd(x), precision=PH), mrb, wu)
    rp = each(lambda r, x: r - x[:, :W2], rm, mw)
    o0 = each(lambda x, y: x[C:] - y[:, W2:], mv, mw)
    bwu = each(lambda b, p, x: _dot(b * p, x, _TN, PH), bm, pt, wu)
    kv = each(lambda k, p, v: _dot(k * p, v, _TN, PH), kp, pt, vm)
    gt = each(lambda p, x: eye_bd * p - jnp.where(bdmask, x[:, :W2], 0.0), pt, bwu)
    ht = each(lambda x, y: jnp.where(bdmask, x - y[:, W2:], 0.0), kv, bwu)
    order = range(n_sub - 1, -1, -1) if reverse else range(n_sub)
    for p in range(n_pairs):
        st = st_s[p]
        for j in order:
            i = j * n_pairs + p
            rs, sl = items[i]
            oc = _dot(rows(rp[i], gt[i]), st, precision=PH)
            o_ref[0, rs, sl] = oc[:C] + o0[i]
            st = oc[C:] + ht[i]
            bon_ref[0, rs, sl] = head_sum(rkd_s[rs, sl]) * vm[i]
        st_s[p] = st

    @pl.when(c == n_steps - 1)
    def _():
        s_ref[0] = st_s[...]


_STATE_BD = (A_HEADS // 2, 2 * A_HEAD, 2 * A_HEAD)


def _state_to_bd(s):
    lead = s.shape[:-3]
    st = jnp.swapaxes(s, -1, -2).reshape(lead + (A_HEADS // 2, 2, A_HEAD, A_HEAD))
    z = jnp.zeros_like(st[..., 0, :, :])
    top = jnp.concatenate([st[..., 0, :, :], z], axis=-1)
    bot = jnp.concatenate([z, st[..., 1, :, :]], axis=-1)
    return jnp.concatenate([top, bot], axis=-2)


def _state_from_bd(st):
    pair = jnp.stack([st[..., :A_HEAD, :A_HEAD], st[..., A_HEAD:, A_HEAD:]], axis=-3)
    return jnp.swapaxes(pair.reshape(st.shape[:-3] + (A_HEADS, A_HEAD, A_HEAD)), -1, -2)


def _rwkv(proj, s0, mu_rkv, mu_lo, w0, w_up, a0, a_up, k_k, k_a, r_k, reverse):
    bsz, t_len, _ = proj.shape
    n_sub = next(n for n in (RWKV_SUB, 2, 1) if t_len % (n * CHUNK) == 0)
    rows = n_sub * CHUNK
    n_steps = t_len // rows
    n_halo = t_len // HALO
    per = rows // HALO
    blk = (lambda c: n_steps - 1 - c) if reverse else (lambda c: c)
    rkv_w = 3 * A_WIDTH
    lo_w = 4 * LORA
    lo_col = (8 * D_MODEL) // lo_w

    def main(width, col):
        return pl.BlockSpec((1, rows, width), lambda b, c: (b, blk(c), col))

    def prev(width, col):
        return pl.BlockSpec((1, HALO, width), lambda b, c: (b, jnp.maximum(blk(c) * per - 1, 0), col))

    def nxt(width, col):
        return pl.BlockSpec((1, HALO, width), lambda b, c: (b, jnp.minimum((blk(c) + 1) * per, n_halo - 1), col))

    vec = lambda width: pl.BlockSpec((1, width), lambda b, c: (0, 0))
    mat = pl.BlockSpec((LORA, A_WIDTH), lambda b, c: (0, 0))
    sspec = pl.BlockSpec((1,) + _STATE_BD, lambda b, c: (b, 0, 0, 0))
    ospec = pl.BlockSpec((1, rows, A_WIDTH), lambda b, c: (b, blk(c), 0))
    big = jax.ShapeDtypeStruct((bsz, t_len, A_WIDTH), F32)
    return pl.pallas_call(
        functools.partial(_rwkv_kernel, n_steps=n_steps, n_sub=n_sub, reverse=reverse),
        grid=(bsz, n_steps),
        in_specs=[main(rkv_w, 0), prev(rkv_w, 0), nxt(rkv_w, 0),
                  main(lo_w, lo_col), prev(lo_w, lo_col), nxt(lo_w, lo_col),
                  vec(rkv_w), vec(lo_w), vec(A_WIDTH), mat, vec(A_WIDTH), mat,
                  vec(A_WIDTH), vec(A_WIDTH), vec(A_WIDTH), sspec],
        out_specs=[ospec, ospec, sspec],
        out_shape=[big, big, jax.ShapeDtypeStruct((bsz,) + _STATE_BD, F32)],
        scratch_shapes=([pltpu.VMEM((rows, A_WIDTH), F32)] * 7
                        + [pltpu.VMEM((n_sub, SUBLANES, A_WIDTH), F32), pltpu.VMEM(_STATE_BD, F32)]),
        compiler_params=_params("parallel", "arbitrary"),
        name="rwkv_bwd" if reverse else "rwkv_fwd",
    )(proj, proj, proj, proj, proj, proj, mu_rkv, mu_lo, w0, w_up, a0, a_up, k_k, k_a, r_k, s0)


def _even_out_kernel(of_ref, ob_ref, bf_ref, bb_ref, ga_ref, yb_ref, x_ref, mod_ref, lw_ref, lb_ref, w_ref,
                     o_ref, y_s):
    o = of_ref[0] + ob_ref[0]
    bon = bf_ref[0] + bb_ref[0]
    ga = ga_ref[0].astype(F32)
    gate_a = ga * jax.nn.sigmoid(ga)
    W2 = 2 * A_HEAD
    low = lax.broadcasted_iota(jnp.int32, (o.shape[0], W2), 1) < A_HEAD

    def head_mean(x):
        s0 = jnp.sum(jnp.where(low, x, 0.0), axis=-1, keepdims=True)
        s1 = jnp.sum(jnp.where(low, 0.0, x), axis=-1, keepdims=True)
        return jnp.where(low, s0, s1) * (1.0 / A_HEAD)

    sls = [slice(p * W2, (p + 1) * W2) for p in range(A_HEADS // 2)]
    cen = [o[:, sl] for sl in sls]
    cen = [x - head_mean(x) for x in cen]
    var = [head_mean(x * x) for x in cen]
    for sl, x, vr in zip(sls, cen, var):
        yh = x * lax.rsqrt(vr + GN_EPS) * lw_ref[:, sl] + lb_ref[:, sl]
        y_s[:, sl] = ((yh + bon[:, sl]) * gate_a[:, sl]).astype(BF16)
    y_s[:, A_WIDTH:] = yb_ref[0]
    out = _dot(y_s[...], w_ref[...])
    gate = mod_ref[0][:, 2 * D_MODEL:]
    o_ref[0] = x_ref[0] + gate * out


def _even_out(o_f, o_b, bon_f, bon_b, proj, yb, x, mods, mod_base, mod_stride, lnx_w, lnx_b, w_out, tm):
    bsz, t_len, _ = x.shape
    row = lambda col: pl.BlockSpec((1, tm, D_MODEL), lambda b, i: (b, i, col))
    vec = pl.BlockSpec((1, A_WIDTH), lambda b, i: (0, 0))
    return pl.pallas_call(
        _even_out_kernel,
        grid=(bsz, t_len // tm),
        in_specs=[row(0), row(0), row(0), row(0), row(3), row(0), row(0),
                  pl.BlockSpec((1, 1, 3 * D_MODEL), lambda b, i: (mod_base + mod_stride * b, 0, 0)),
                  vec, vec, pl.BlockSpec((A_WIDTH + B_WIDTH, D_MODEL), lambda b, i: (0, 0))],
        out_specs=row(0),
        out_shape=jax.ShapeDtypeStruct(x.shape, F32),
        scratch_shapes=[pltpu.VMEM((tm, A_WIDTH + B_WIDTH), BF16)],
        compiler_params=_params("parallel", "parallel"),
        name="even_out",
    )(o_f, o_b, bon_f, bon_b, proj, yb, x, mods, lnx_w, lnx_b, w_out)


def _odd_out_kernel(bg_ref, cg_ref, u_ref, z_ref, cgp_ref, up_ref, cgn_ref, un_ref, cw_ref, cb_ref,
                    x_ref, mod_ref, w_ref, o_ref):
    i = pl.program_id(1)
    tm = bg_ref.shape[1]
    row = lax.broadcasted_iota(jnp.int32, (tm, 1), 0)
    f32 = lambda ref: ref[0].astype(F32)
    cu = f32(cg_ref) * f32(u_ref)
    prev_row = jnp.where(i == 0, 0.0, (f32(cgp_ref) * f32(up_ref))[HALO - 1:HALO, :])
    next_row = jnp.where(i == pl.num_programs(1) - 1, 0.0, (f32(cgn_ref) * f32(un_ref))[0:1, :])
    cu_prev = jnp.where(row == 0, prev_row, pltpu.roll(cu, 1, 0))
    cu_next = jnp.where(row == tm - 1, next_row, pltpu.roll(cu, tm - 1, 0))
    cw = cw_ref[...]
    conv = cu_prev * cw[0:1] + cu * cw[1:2] + cu_next * cw[2:3] + cb_ref[...]
    z = f32(z_ref)
    y = f32(bg_ref) * conv * (z * jax.nn.sigmoid(z))
    out = _dot(y.astype(BF16), w_ref[...])
    gate = mod_ref[0][:, 2 * D_MODEL:]
    o_ref[0] = x_ref[0] + gate * out


def _odd_out(proj, x, mods, mod_base, mod_stride, conv_w, conv_b, w_out, tm):
    bsz, t_len, _ = x.shape
    per = tm // HALO
    n_halo = t_len // HALO
    main = lambda col: pl.BlockSpec((1, tm, C_WIDTH), lambda b, i: (b, i, col))
    prev = lambda col: pl.BlockSpec((1, HALO, C_WIDTH), lambda b, i: (b, jnp.maximum(i * per - 1, 0), col))
    nxt = lambda col: pl.BlockSpec((1, HALO, C_WIDTH),
                                   lambda b, i: (b, jnp.minimum((i + 1) * per, n_halo - 1), col))
    xspec = pl.BlockSpec((1, tm, D_MODEL), lambda b, i: (b, i, 0))
    return pl.pallas_call(
        _odd_out_kernel,
        grid=(bsz, t_len // tm),
        in_specs=[main(0), main(1), main(2), main(3), prev(1), prev(2), nxt(1), nxt(2),
                  pl.BlockSpec(conv_w.shape, lambda b, i: (0, 0)),
                  pl.BlockSpec((1, C_WIDTH), lambda b, i: (0, 0)),
                  xspec,
                  pl.BlockSpec((1, 1, 3 * D_MODEL), lambda b, i: (mod_base + mod_stride * b, 0, 0)),
                  pl.BlockSpec((C_WIDTH, D_MODEL), lambda b, i: (0, 0))],
        out_specs=xspec,
        out_shape=jax.ShapeDtypeStruct(x.shape, F32),
        compiler_params=_params("parallel", "parallel"),
        name="odd_out",
    )(proj, proj, proj, proj, proj, proj, proj, proj, conv_w, conv_b, x, mods, w_out)


def _rope_tables(t_len):
    pos = jnp.arange(t_len, dtype=jnp.int32)
    row = (pos // GRID_W).astype(F32)
    col = (pos % GRID_W).astype(F32)
    half = B_HD // 2
    inv = ROPE_BASE ** (-jnp.arange(0, half, 2, dtype=F32) / half)
    ar = row[:, None] * inv
    ac = col[:, None] * inv
    cos = jnp.concatenate([jnp.cos(ar), jnp.cos(ar), jnp.cos(ac), jnp.cos(ac)], axis=-1)
    sin = jnp.concatenate([-jnp.sin(ar), jnp.sin(ar), -jnp.sin(ac), jnp.sin(ac)], axis=-1)
    return jnp.tile(cos, (1, 2)), jnp.tile(sin, (1, 2))


def _proj_rows(x, norm_w, mods, mod_base, mod_stride, w, flatten):
    bsz, t_len, _ = x.shape
    if flatten:
        x = x.reshape(1, bsz * t_len, D_MODEL)
    rows = x.shape[1]
    tm = next(c for c in (2048, 1024, 512, 256, 128) if rows % c == 0)
    out = _in_proj(x, norm_w, mods, mod_base, mod_stride, w, tm)
    return out.reshape(bsz, t_len, -1)


def _even_layer(x, mods, mod_base, mod_stride, flatten, norm_w, w_in, w_out, rw, dp, lam_init,
                s0, ctx_k, ctx_v, rope, emit_cache):
    bsz, t_len, _ = x.shape
    proj = _proj_rows(x, norm_w, mods, mod_base, mod_stride, w_in, flatten)
    mu_rkv, mu_lo, w0, w_up, a0, a_up, k_k, k_a, r_k, lnx_w, lnx_b = rw
    q_norm2, k_norm2, lam_vec, subln = dp
    tm = min(t_len, 256)
    cos, sin = rope if rope is not None else (None, None)
    prep = _qk_prep(proj, q_norm2, k_norm2, cos, sin, tm, emit_cache)
    qn, kn = prep[:2]
    yb = _attention(lam_vec, qn, kn, ctx_k, ctx_v, proj, subln, lam_init, min(t_len, 512))
    scan = lambda d: _rwkv(proj, s0[d], mu_rkv, mu_lo, w0[d:d + 1], w_up[d], a0[d:d + 1], a_up[d], k_k, k_a, r_k,
                           reverse=(d == 1))
    o_f, bon_f, s_f = scan(0)
    o_b, bon_b, s_b = scan(1)
    s_new = jnp.stack([s_f, s_b], axis=0)
    x_new = _even_out(o_f, o_b, bon_f, bon_b, proj, yb, x, mods, mod_base, mod_stride, lnx_w, lnx_b, w_out, tm)
    k_cache = prep[2] if emit_cache else None
    v_cache = proj[:, :, 6 * D_MODEL:7 * D_MODEL].astype(F32) if emit_cache else None
    return x_new, s_new, k_cache, v_cache


def _odd_layer(x, mods, mod_base, mod_stride, flatten, norm_w, w_in, conv_w, conv_b, w_out):
    proj = _proj_rows(x, norm_w, mods, mod_base, mod_stride, w_in, flatten)
    tm = min(x.shape[1], 256)
    return _odd_out(proj, x, mods, mod_base, mod_stride, conv_w, conv_b, w_out, tm)


def kernel(x_prompt, x_sample, state_rwkv_fwd, state_rwkv_bwd, cache_diff_k, cache_diff_v, c, c_ctx,
           norm_w, ada_w, ada_b,
           e_w_in, e_w_out, e_mu, e_w0, e_w_up, e_a0, e_a_up, e_k_k, e_k_a, e_r_k, e_lnx_w, e_lnx_b,
           e_q_norm, e_k_norm, e_lambda, e_subln,
           o_w_in, o_conv_w, o_conv_b, o_w_out):
    depth = norm_w.shape[0]
    bsz, seq = x_prompt.shape[:2]
    dec_b, dec_t = x_sample.shape[:2]
    past = cache_diff_k.shape[2]
    assert dec_b + 1 <= MOD_ROWS
    cvec = jnp.concatenate([c_ctx[None, :], c, jnp.zeros((MOD_ROWS - 1 - dec_b, D_MODEL), F32)], axis=0)
    mods = _ada(cvec, ada_w, ada_b)
    rope = _rope_tables(dec_t)
    rkv_w = 3 * A_WIDTH
    lora_end = rkv_w + 4 * LORA

    xp, xs = x_prompt, x_sample
    new_sf, new_sb, new_k, new_v = [], [], [], []
    for layer in range(depth):
        i = layer // 2
        nw = norm_w[layer][None, :]
        base = layer * MOD_ROWS
        if layer % 2 == 0:
            lam_init = 0.8 - 0.6 * math.exp(-0.3 * layer)
            w_in = jnp.concatenate([e_w_in[i][:, :rkv_w], e_w_in[i][:, lora_end:], e_w_in[i][:, rkv_w:lora_end]],
                                   axis=1).astype(BF16)
            w_out = e_w_out[i].astype(BF16)
            row = lambda t: t.reshape(1, -1)
            rw = (row(e_mu[i][:rkv_w]), row(e_mu[i][rkv_w:lora_end]), e_w0[i], e_w_up[i], e_a0[i], e_a_up[i],
                  row(e_k_k[i]), row(e_k_a[i]), row(e_r_k[i]), row(e_lnx_w[i]), row(e_lnx_b[i]))
            dp = (jnp.tile(e_q_norm[i], 2)[None, :], jnp.tile(e_k_norm[i], 2)[None, :], e_lambda[i],
                  row(e_subln[i]))
            zeros = jnp.zeros((2, bsz) + _STATE_BD, F32)
            xp, s_p, k_c, v_c = _even_layer(xp, mods, base, 0, True, nw, w_in, w_out, rw, dp, lam_init,
                                            zeros, None, None, None, True)
            s_p = _state_from_bd(s_p)
            s0 = _state_to_bd(jnp.stack([state_rwkv_fwd[:, i], state_rwkv_bwd[:, i]], axis=0))
            ctx_k = cache_diff_k[:, i].reshape(dec_b, past, B_WIDTH).astype(BF16)
            ctx_v = cache_diff_v[:, i].reshape(dec_b, past, B_WIDTH).astype(BF16)
            xs, _, _, _ = _even_layer(xs, mods, base + 1, 1, False, nw, w_in, w_out, rw, dp, lam_init,
                                      s0, ctx_k, ctx_v, rope, False)
            new_sf.append(s_p[0])
            new_sb.append(s_p[1])
            new_k.append(k_c.reshape(bsz, seq, B_HEADS, 2, B_HD))
            new_v.append(v_c.reshape(bsz, seq, B_HEADS, 2 * B_HD))
        else:
            w_in = o_w_in[i].astype(BF16)
            w_out = o_w_out[i].astype(BF16)
            cb = o_conv_b[i][None, :]
            xp = _odd_layer(xp, mods, base, 0, True, nw, w_in, o_conv_w[i], cb, w_out)
            xs = _odd_layer(xs, mods, base + 1, 1, False, nw, w_in, o_conv_w[i], cb, w_out)
    return (xp, xs, jnp.stack(new_sf, axis=1), jnp.stack(new_sb, axis=1),
            jnp.stack(new_k, axis=1), jnp.stack(new_v, axis=1))
```

```python
import functools
import math

import jax
import jax.numpy as jnp
from jax import lax
from jax.experimental import pallas as pl
from jax.experimental.pallas import tpu as pltpu

F32 = jnp.float32
BF16 = jnp.bfloat16
HI = lax.Precision.HIGHEST
PH = None

D_MODEL = 1024
A_WIDTH = 1024
A_HEAD = 64
A_HEADS = A_WIDTH // A_HEAD
LORA = 64
B_HD = 64
B_HEADS = 8
B_WIDTH = 1024
C_WIDTH = 2048
GRID_W = 64
ROPE_BASE = 10000.0
NORM_EPS = 1e-6
GN_EPS = 64e-5
MOD_ROWS = 8
CHUNK = 64
RWKV_SUB = 4
SUBLANES = 8
HALO = 16
VMEM_LIMIT = 48 * 1024 * 1024
ATTN_ROWS = 128

_NT = (((1,), (1,)), ((), ()))
_TN = (((0,), (0,)), ((), ()))


def _dot(x, y, dims=None, precision=None):
    if dims is None:
        return jnp.dot(x, y, precision=precision, preferred_element_type=F32)
    return lax.dot_general(x, y, dims, precision=precision, preferred_element_type=F32)


def _params(*sem):
    return pltpu.CompilerParams(dimension_semantics=sem, vmem_limit_bytes=VMEM_LIMIT)


def _ada_kernel(c_ref, w_ref, b_ref, o_ref):
    cv = c_ref[...]
    s = cv * jax.nn.sigmoid(cv)
    o_ref[0] = _dot(s, w_ref[0], precision=HI) + b_ref[0]


def _ada(cvec, ada_w, ada_b):
    depth = ada_w.shape[0]
    tn = D_MODEL
    out = pl.pallas_call(
        _ada_kernel,
        grid=(depth, 3 * D_MODEL // tn),
        in_specs=[pl.BlockSpec((MOD_ROWS, D_MODEL), lambda l, j: (0, 0)),
                  pl.BlockSpec((1, D_MODEL, tn), lambda l, j: (l, 0, j)),
                  pl.BlockSpec((1, 1, tn), lambda l, j: (l, 0, j))],
        out_specs=pl.BlockSpec((1, MOD_ROWS, tn), lambda l, j: (l, 0, j)),
        out_shape=jax.ShapeDtypeStruct((depth, MOD_ROWS, 3 * D_MODEL), F32),
        compiler_params=_params("parallel", "parallel"),
        name="ada",
    )(cvec, ada_w, ada_b.reshape(depth, 1, 3 * D_MODEL))
    return out.reshape(depth * MOD_ROWS, 1, 3 * D_MODEL)


def _in_proj_kernel(x_ref, nw_ref, mod_ref, w_ref, o_ref, h_ref):
    @pl.when(pl.program_id(2) == 0)
    def _():
        x = x_ref[0]
        ms = jnp.mean(x * x, axis=-1, keepdims=True)
        y = x * lax.rsqrt(ms + NORM_EPS) * nw_ref[...]
        m = mod_ref[0]
        h = y * (1.0 + m[:, D_MODEL:2 * D_MODEL]) + m[:, :D_MODEL]
        h_ref[...] = h.astype(BF16)

    o_ref[0] = _dot(h_ref[...], w_ref[...]).astype(o_ref.dtype)


def _in_proj(x, norm_w, mods, mod_base, mod_stride, w, tm):
    bsz, t_len, _ = x.shape
    n_out = w.shape[1]
    tn = next(c for c in (1024, 768, 512, 256, 128) if n_out % c == 0)
    return pl.pallas_call(
        _in_proj_kernel,
        grid=(bsz, t_len // tm, n_out // tn),
        in_specs=[pl.BlockSpec((1, tm, D_MODEL), lambda b, i, j: (b, i, 0)),
                  pl.BlockSpec((1, D_MODEL), lambda b, i, j: (0, 0)),
                  pl.BlockSpec((1, 1, 3 * D_MODEL), lambda b, i, j: (mod_base + mod_stride * b, 0, 0)),
                  pl.BlockSpec((D_MODEL, tn), lambda b, i, j: (0, j))],
        out_specs=pl.BlockSpec((1, tm, tn), lambda b, i, j: (b, i, j)),
        out_shape=jax.ShapeDtypeStruct((bsz, t_len, n_out), BF16),
        scratch_shapes=[pltpu.VMEM((tm, D_MODEL), BF16)],
        compiler_params=_params("parallel", "parallel", "arbitrary"),
        name="in_proj",
    )(x, norm_w, mods, w)


def _qk_prep_kernel(*refs, rope, emit_cache):
    pq_ref, pk_ref, qn_ref, kn_ref = refs[:4]
    pos = 4
    if rope:
        cos_ref, sin_ref = refs[pos:pos + 2]
        pos += 2
    q_out, k_out = refs[pos:pos + 2]
    kc_out = refs[pos + 2] if emit_cache else None
    width = 2 * B_HD
    tm = pq_ref.shape[1]
    lane = lax.broadcasted_iota(jnp.int32, (tm, width), 1)
    low = lane < B_HD
    first16 = (lane % 32) < 16

    def norm(x, w):
        ss = x * x
        s0 = jnp.sum(jnp.where(low, ss, 0.0), axis=-1, keepdims=True)
        s1 = jnp.sum(jnp.where(low, 0.0, ss), axis=-1, keepdims=True)
        ms = jnp.where(low, s0, s1) * (1.0 / B_HD)
        return x * lax.rsqrt(ms + NORM_EPS) * w

    def rot(x):
        if not rope:
            return x
        partner = jnp.where(first16, pltpu.roll(x, width - 16, 1), pltpu.roll(x, 16, 1))
        return x * cos_ref[...] + partner * sin_ref[...]

    for h in range(B_HEADS):
        sl = slice(h * width, (h + 1) * width)
        q = norm(pq_ref[0, :, sl].astype(F32), qn_ref[...])
        k = norm(pk_ref[0, :, sl].astype(F32), kn_ref[...])
        if emit_cache:
            kc_out[0, :, sl] = k
        q_out[0, :, sl] = (rot(q) * (B_HD ** -0.5 * math.log2(math.e))).astype(BF16)
        k_out[0, :, sl] = rot(k).astype(BF16)


def _qk_prep(proj, q_norm2, k_norm2, cos, sin, tm, emit_cache):
    bsz, t_len, _ = proj.shape
    rope = cos is not None
    col = lambda c: pl.BlockSpec((1, tm, B_WIDTH), lambda b, i: (b, i, c))
    vec = pl.BlockSpec((1, 2 * B_HD), lambda b, i: (0, 0))
    in_specs = [col(4), col(5), vec, vec]
    args = [proj, proj, q_norm2, k_norm2]
    if rope:
        tab = pl.BlockSpec((tm, 2 * B_HD), lambda b, i: (i, 0))
        in_specs += [tab, tab]
        args += [cos, sin]
    out_spec = pl.BlockSpec((1, tm, B_WIDTH), lambda b, i: (b, i, 0))
    out_shape = [jax.ShapeDtypeStruct((bsz, t_len, B_WIDTH), BF16)] * 2
    out_specs = [out_spec] * 2
    if emit_cache:
        out_shape = out_shape + [jax.ShapeDtypeStruct((bsz, t_len, B_WIDTH), F32)]
        out_specs = out_specs + [out_spec]
    return pl.pallas_call(
        functools.partial(_qk_prep_kernel, rope=rope, emit_cache=emit_cache),
        grid=(bsz, t_len // tm),
        in_specs=in_specs, out_specs=out_specs, out_shape=out_shape,
        compiler_params=_params("parallel", "parallel"),
        name="qk_prep",
    )(*args)


def _attn_kernel(*refs, lam_init, has_ctx):
    lam_ref, q_ref, k_ref, v_ref = refs[:4]
    pos = 4
    if has_ctx:
        ck_ref, cv_ref = refs[pos:pos + 2]
        pos += 2
    gb_ref, sub_ref, o_ref, k_s, v_s = refs[pos:pos + 5]
    width = 2 * B_HD
    t_len = k_ref.shape[1]
    tq = q_ref.shape[1]

    @pl.when(pl.program_id(2) == 0)
    def _():
        k_s[0:t_len, :] = k_ref[0]
        v_s[0:t_len, 0:width] = v_ref[0]
        if has_ctx:
            k_s[t_len:, :] = ck_ref[0]
            v_s[t_len:, 0:width] = cv_ref[0]
        v_s[:, width:] = jnp.ones((v_s.shape[0], width), BF16)

    q = q_ref[0]
    lane = lax.broadcasted_iota(jnp.int32, q.shape, 1)
    zero = jnp.zeros_like(q)
    q2 = jnp.concatenate([jnp.where(lane < B_HD, q, zero), jnp.where(lane < B_HD, zero, q)], axis=0)
    n_groups = 2 * tq // ATTN_ROWS
    scores = lambda g: _dot(q2[g * ATTN_ROWS:(g + 1) * ATTN_ROWS], k_s[...], _NT)

    def softmax_pv(s):
        m = jnp.max(s, axis=-1, keepdims=True)
        acc = _dot(jnp.exp2(s - m).astype(BF16), v_s[...])
        return acc[:, :width] / acc[:, width:]

    parts = []
    s_next = scores(0)
    for g in range(n_groups):
        s_cur = s_next
        if g + 1 < n_groups:
            s_next = scores(g + 1)
        parts.append(softmax_pv(s_cur))
    comp = jnp.concatenate(parts, axis=0)

    lv = lam_ref[...]
    lam = (jnp.exp(jnp.sum(lv[0:1] * lv[1:2], axis=-1, keepdims=True))
           - jnp.exp(jnp.sum(lv[2:3] * lv[3:4], axis=-1, keepdims=True)) + lam_init)
    o = comp[:tq] - lam * comp[tq:]
    ms = jnp.mean(o * o, axis=-1, keepdims=True)
    o = o * lax.rsqrt(ms + NORM_EPS) * sub_ref[...] * (1.0 - lam_init)
    gb = gb_ref[0].astype(F32)
    o_ref[0] = (o * (gb * jax.nn.sigmoid(gb))).astype(BF16)


def _attention(lam_vec, q, k, ctx_k, ctx_v, proj, subln, lam_init, tq):
    bsz, t_len, _ = q.shape
    has_ctx = ctx_k is not None
    width = 2 * B_HD
    qspec = pl.BlockSpec((1, tq, width), lambda b, h, i: (b, i, h))
    kspec = pl.BlockSpec((1, t_len, width), lambda b, h, i: (b, 0, h))
    vspec = pl.BlockSpec((1, t_len, width), lambda b, h, i: (b, 0, 6 * B_HEADS + h))
    in_specs = [pl.BlockSpec((4, B_HD), lambda b, h, i: (0, 0)), qspec, kspec, vspec]
    args = [lam_vec, q, k, proj]
    if has_ctx:
        cspec = pl.BlockSpec((1, ctx_k.shape[1], width), lambda b, h, i: (b, 0, h))
        in_specs += [cspec, cspec]
        args += [ctx_k, ctx_v]
    in_specs += [pl.BlockSpec((1, tq, width), lambda b, h, i: (b, i, 7 * B_HEADS + h)),
                 pl.BlockSpec((1, width), lambda b, h, i: (0, 0))]
    args += [proj, subln]
    n_keys = t_len + (ctx_k.shape[1] if has_ctx else 0)
    return pl.pallas_call(
        functools.partial(_attn_kernel, lam_init=lam_init, has_ctx=has_ctx),
        grid=(bsz, B_HEADS, t_len // tq),
        in_specs=in_specs, out_specs=qspec,
        out_shape=jax.ShapeDtypeStruct((bsz, t_len, B_WIDTH), BF16),
        scratch_shapes=[pltpu.VMEM((n_keys, width), BF16), pltpu.VMEM((n_keys, 2 * width), BF16)],
        compiler_params=_params("parallel", "parallel", "arbitrary"),
        name="diff_attn",
    )(*args)


def _rwkv_kernel(rkv_ref, rkv_p_ref, rkv_n_ref, lo_ref, lo_p_ref, lo_n_ref, mu_rkv_ref, mu_lo_ref,
                 w0_ref, wup_ref, a0_ref, aup_ref, kk_ref, ka_ref, rk_ref, s0_ref,
                 o_ref, bon_ref, s_ref,
                 xpad_s, st_s, *, n_steps, n_sub, reverse):
    C = CHUNK
    R = n_sub * C
    PAD = SUBLANES
    c = pl.program_id(1)
    blk = n_steps - 1 - c if reverse else c
    first = blk == 0
    last = blk == n_steps - 1
    row = lax.broadcasted_iota(jnp.int32, (R, 1), 0)
    up = lambda ref: ref[0].astype(F32)

    lo = up(lo_ref)
    lo_prev = jnp.where(first, 0.0, up(lo_p_ref)[HALO - 1:HALO, :])
    lo_next = jnp.where(last, 0.0, up(lo_n_ref)[0:1, :])
    lo_p = jnp.where(row == 0, lo_prev, pltpu.roll(lo, 1, 0))
    lo_n = jnp.where(row == R - 1, lo_next, pltpu.roll(lo, R - 1, 0))
    lo = lo + mu_lo_ref[...] * (0.5 * (lo_p + lo_n) - lo)
    dcol = 1 if reverse else 0
    wl = lo[:, dcol * LORA:(dcol + 1) * LORA]
    al = lo[:, (2 + dcol) * LORA:(3 + dcol) * LORA]
    xw = w0_ref[...] + _dot(jnp.tanh(wl), wup_ref[...], precision=PH)
    logw = -math.exp(-0.5) * jax.nn.sigmoid(xw)
    a = jax.nn.sigmoid(a0_ref[...] + _dot(al, aup_ref[...], precision=PH))

    xpad_s[PAD:PAD + R, :] = up(rkv_ref)
    xpad_s[PAD - 1:PAD, :] = jnp.where(first, 0.0, up(rkv_p_ref)[HALO - 1:HALO, :])
    xpad_s[PAD + R:PAD + R + 1, :] = jnp.where(last, 0.0, up(rkv_n_ref)[0:1, :])

    ti = lax.broadcasted_iota(jnp.int32, (C, C), 0)
    si = lax.broadcasted_iota(jnp.int32, (C, C), 1)
    incl = (si >= ti) if reverse else (si <= ti)
    tri = jnp.where(incl, 1.0, 0.0).astype(BF16)

    @pl.when(c == 0)
    def _():
        st_s[...] = s0_ref[0]

    W2 = 2 * A_HEAD
    lane = lax.broadcasted_iota(jnp.int32, (C, W2), 1)
    trow = lax.broadcasted_iota(jnp.int32, (C, W2), 0)
    low = lane < A_HEAD
    before2 = (lane % A_HEAD - trow) if reverse else (trow - lane % A_HEAD)
    strict2 = before2 > 0
    incl2 = before2 >= 0
    eye2 = jnp.where(before2 == 0, 1.0, 0.0)
    brow = lax.broadcasted_iota(jnp.int32, (W2, W2), 0)
    bcol = lax.broadcasted_iota(jnp.int32, (W2, W2), 1)
    bdmask = (brow < A_HEAD) == (bcol < A_HEAD)
    bdmask2 = jnp.concatenate([bdmask, bdmask], axis=1)
    eye_bd = jnp.where(brow == bcol, 1.0, 0.0)

    def bd(x):
        return jnp.where(bdmask if x.shape[1] == W2 else bdmask2, jnp.concatenate([x, x], axis=0), 0.0)

    def head_sum(x):
        s0 = jnp.sum(jnp.where(low, x, 0.0), axis=-1, keepdims=True)
        s1 = jnp.sum(jnp.where(low, 0.0, x), axis=-1, keepdims=True)
        return jnp.where(low, s0, s1)

    def rows(*xs):
        return jnp.concatenate(xs, axis=0)

    n_pairs = A_HEADS // 2
    state = [st_s[p] for p in range(n_pairs)]

    def item(j, p):
        rs = slice(j * C, (j + 1) * C)
        sl = slice(p * W2, (p + 1) * W2)

        def shifted(col0):
            cs = slice(col0 + p * W2, col0 + (p + 1) * W2)
            base = PAD + j * C
            x = xpad_s[base:base + C, cs]
            nb = xpad_s[base - 1:base - 1 + C, cs] + xpad_s[base + 1:base + 1 + C, cs]
            return x + mu_rkv_ref[:, cs] * (0.5 * nb - x)

        r = shifted(0)
        k = shifted(A_WIDTH)
        vm = shifted(2 * A_WIDTH)
        lw = logw[rs, sl]
        av = a[rs, sl]
        kd = k * (1.0 + (av - 1.0) * ka_ref[:, sl])
        kk = k * kk_ref[:, sl]
        kk = kk / jnp.maximum(jnp.sqrt(head_sum(kk * kk)), 1e-12)
        bon_ref[0, rs, sl] = head_sum(r * kd * rk_ref[:, sl]) * vm
        t0 = lw.astype(BF16)
        e0 = lw - t0.astype(F32)
        t1 = e0.astype(BF16)
        t2 = (e0 - t1.astype(F32)).astype(BF16)
        cum = _dot(tri, t0) + _dot(tri, t1) + _dot(tri, t2)
        yield
        pinv = jnp.exp(-cum)
        am = kk * jnp.exp(cum - lw)
        bm = kk * av * pinv
        kp = kd * pinv
        rm = r * jnp.exp(cum)
        pt = jnp.exp(jnp.sum(lw, axis=0, keepdims=True))
        ar = rows(am, rm)
        xb = _dot(ar, bd(bm), _NT, PH)
        xk = _dot(ar, bd(kp), _NT, PH)
        yield
        nab = jnp.where(strict2, -xb[:C], 0.0)
        mak = jnp.where(strict2, xk[:C], 0.0)
        mrb = jnp.where(incl2, xb[C:], 0.0)
        mrk = jnp.where(incl2, xk[C:], 0.0)
        tinv = eye2 + nab
        pw = _dot(nab, bd(nab), precision=PH)
        mv = _dot(rows(mak, mrk), bd(vm), precision=PH)
        yield
        for _ in range(int(math.log2(C)) - 2):
            both = _dot(rows(tinv, pw), bd(pw), precision=PH)
            tinv = tinv + both[:C]
            pw = both[C:]
            yield
        tinv = tinv + _dot(tinv, bd(pw), precision=PH)
        yield
        wu = jnp.concatenate([_dot(tinv, bd(am), precision=PH), _dot(tinv, bd(mv[:C]), precision=PH)], axis=1)
        yield
        mw = _dot(mrb, bd(wu), precision=PH)
        bwu = _dot(bm * pt, wu, _TN, PH)
        kv = _dot(kp * pt, vm, _TN, PH)
        yield
        rp = rm - mw[:, :W2]
        o0 = mv[C:] - mw[:, W2:]
        gt = eye_bd * pt - jnp.where(bdmask, bwu[:, :W2], 0.0)
        ht = jnp.where(bdmask, kv - bwu[:, W2:], 0.0)
        oc = _dot(rows(rp, gt), state[p], precision=PH)
        o_ref[0, rs, sl] = oc[:C] + o0
        state[p] = oc[C:] + ht

    order = range(n_sub - 1, -1, -1) if reverse else range(n_sub)
    pending = [item(j, p) for j in order for p in range(n_pairs)]
    active = []
    while pending or active:
        if pending:
            active.append(pending.pop(0))
        for gen in list(active):
            if next(gen, "done") == "done":
                active.remove(gen)
    for p in range(n_pairs):
        st_s[p] = state[p]

    @pl.when(c == n_steps - 1)
    def _():
        s_ref[0] = st_s[...]


_STATE_BD = (A_HEADS // 2, 2 * A_HEAD, 2 * A_HEAD)


def _state_to_bd(s):
    lead = s.shape[:-3]
    st = jnp.swapaxes(s, -1, -2).reshape(lead + (A_HEADS // 2, 2, A_HEAD, A_HEAD))
    z = jnp.zeros_like(st[..., 0, :, :])
    top = jnp.concatenate([st[..., 0, :, :], z], axis=-1)
    bot = jnp.concatenate([z, st[..., 1, :, :]], axis=-1)
    return jnp.concatenate([top, bot], axis=-2)


def _state_from_bd(st):
    pair = jnp.stack([st[..., :A_HEAD, :A_HEAD], st[..., A_HEAD:, A_HEAD:]], axis=-3)
    return jnp.swapaxes(pair.reshape(st.shape[:-3] + (A_HEADS, A_HEAD, A_HEAD)), -1, -2)


def _rwkv(proj, s0, mu_rkv, mu_lo, w0, w_up, a0, a_up, k_k, k_a, r_k, reverse):
    bsz, t_len, _ = proj.shape
    n_sub = next(n for n in (RWKV_SUB, 2, 1) if t_len % (n * CHUNK) == 0)
    rows = n_sub * CHUNK
    n_steps = t_len // rows
    n_halo = t_len // HALO
    per = rows // HALO
    blk = (lambda c: n_steps - 1 - c) if reverse else (lambda c: c)
    rkv_w = 3 * A_WIDTH
    lo_w = 4 * LORA
    lo_col = (8 * D_MODEL) // lo_w

    def main(width, col):
        return pl.BlockSpec((1, rows, width), lambda b, c: (b, blk(c), col))

    def prev(width, col):
        return pl.BlockSpec((1, HALO, width), lambda b, c: (b, jnp.maximum(blk(c) * per - 1, 0), col))

    def nxt(width, col):
        return pl.BlockSpec((1, HALO, width), lambda b, c: (b, jnp.minimum((blk(c) + 1) * per, n_halo - 1), col))

    vec = lambda width: pl.BlockSpec((1, width), lambda b, c: (0, 0))
    mat = pl.BlockSpec((LORA, A_WIDTH), lambda b, c: (0, 0))
    sspec = pl.BlockSpec((1,) + _STATE_BD, lambda b, c: (b, 0, 0, 0))
    ospec = pl.BlockSpec((1, rows, A_WIDTH), lambda b, c: (b, blk(c), 0))
    big = jax.ShapeDtypeStruct((bsz, t_len, A_WIDTH), F32)
    return pl.pallas_call(
        functools.partial(_rwkv_kernel, n_steps=n_steps, n_sub=n_sub, reverse=reverse),
        grid=(bsz, n_steps),
        in_specs=[main(rkv_w, 0), prev(rkv_w, 0), nxt(rkv_w, 0),
                  main(lo_w, lo_col), prev(lo_w, lo_col), nxt(lo_w, lo_col),
                  vec(rkv_w), vec(lo_w), vec(A_WIDTH), mat, vec(A_WIDTH), mat,
                  vec(A_WIDTH), vec(A_WIDTH), vec(A_WIDTH), sspec],
        out_specs=[ospec, ospec, sspec],
        out_shape=[big, big, jax.ShapeDtypeStruct((bsz,) + _STATE_BD, F32)],
        scratch_shapes=[pltpu.VMEM((rows + 2 * SUBLANES, rkv_w), F32), pltpu.VMEM(_STATE_BD, F32)],
        compiler_params=_params("parallel", "arbitrary"),
        name="rwkv_bwd" if reverse else "rwkv_fwd",
    )(proj, proj, proj, proj, proj, proj, mu_rkv, mu_lo, w0, w_up, a0, a_up, k_k, k_a, r_k, s0)


def _even_out_kernel(of_ref, ob_ref, bf_ref, bb_ref, ga_ref, yb_ref, x_ref, mod_ref, lw_ref, lb_ref, w_ref,
                     o_ref, y_s):
    o = of_ref[0] + ob_ref[0]
    bon = bf_ref[0] + bb_ref[0]
    ga = ga_ref[0].astype(F32)
    gate_a = ga * jax.nn.sigmoid(ga)
    W2 = 2 * A_HEAD
    low = lax.broadcasted_iota(jnp.int32, (o.shape[0], W2), 1) < A_HEAD

    def head_mean(x):
        s0 = jnp.sum(jnp.where(low, x, 0.0), axis=-1, keepdims=True)
        s1 = jnp.sum(jnp.where(low, 0.0, x), axis=-1, keepdims=True)
        return jnp.where(low, s0, s1) * (1.0 / A_HEAD)

    sls = [slice(p * W2, (p + 1) * W2) for p in range(A_HEADS // 2)]
    cen = [o[:, sl] for sl in sls]
    cen = [x - head_mean(x) for x in cen]
    var = [head_mean(x * x) for x in cen]
    for sl, x, vr in zip(sls, cen, var):
        yh = x * lax.rsqrt(vr + GN_EPS) * lw_ref[:, sl] + lb_ref[:, sl]
        y_s[:, sl] = ((yh + bon[:, sl]) * gate_a[:, sl]).astype(BF16)
    y_s[:, A_WIDTH:] = yb_ref[0]
    out = _dot(y_s[...], w_ref[...])
    gate = mod_ref[0][:, 2 * D_MODEL:]
    o_ref[0] = x_ref[0] + gate * out


def _even_out(o_f, o_b, bon_f, bon_b, proj, yb, x, mods, mod_base, mod_stride, lnx_w, lnx_b, w_out, tm):
    bsz, t_len, _ = x.shape
    row = lambda col: pl.BlockSpec((1, tm, D_MODEL), lambda b, i: (b, i, col))
    vec = pl.BlockSpec((1, A_WIDTH), lambda b, i: (0, 0))
    return pl.pallas_call(
        _even_out_kernel,
        grid=(bsz, t_len // tm),
        in_specs=[row(0), row(0), row(0), row(0), row(3), row(0), row(0),
                  pl.BlockSpec((1, 1, 3 * D_MODEL), lambda b, i: (mod_base + mod_stride * b, 0, 0)),
                  vec, vec, pl.BlockSpec((A_WIDTH + B_WIDTH, D_MODEL), lambda b, i: (0, 0))],
        out_specs=row(0),
        out_shape=jax.ShapeDtypeStruct(x.shape, F32),
        scratch_shapes=[pltpu.VMEM((tm, A_WIDTH + B_WIDTH), BF16)],
        compiler_params=_params("parallel", "parallel"),
        name="even_out",
    )(o_f, o_b, bon_f, bon_b, proj, yb, x, mods, lnx_w, lnx_b, w_out)


def _odd_out_kernel(bg_ref, cg_ref, u_ref, z_ref, cgp_ref, up_ref, cgn_ref, un_ref, cw_ref, cb_ref,
                    x_ref, mod_ref, w_ref, o_ref):
    i = pl.program_id(1)
    tm = bg_ref.shape[1]
    row = lax.broadcasted_iota(jnp.int32, (tm, 1), 0)
    f32 = lambda ref: ref[0].astype(F32)
    cu = f32(cg_ref) * f32(u_ref)
    prev_row = jnp.where(i == 0, 0.0, (f32(cgp_ref) * f32(up_ref))[HALO - 1:HALO, :])
    next_row = jnp.where(i == pl.num_programs(1) - 1, 0.0, (f32(cgn_ref) * f32(un_ref))[0:1, :])
    cu_prev = jnp.where(row == 0, prev_row, pltpu.roll(cu, 1, 0))
    cu_next = jnp.where(row == tm - 1, next_row, pltpu.roll(cu, tm - 1, 0))
    cw = cw_ref[...]
    conv = cu_prev * cw[0:1] + cu * cw[1:2] + cu_next * cw[2:3] + cb_ref[...]
    z = f32(z_ref)
    y = f32(bg_ref) * conv * (z * jax.nn.sigmoid(z))
    out = _dot(y.astype(BF16), w_ref[...])
    gate = mod_ref[0][:, 2 * D_MODEL:]
    o_ref[0] = x_ref[0] + gate * out


def _odd_out(proj, x, mods, mod_base, mod_stride, conv_w, conv_b, w_out, tm):
    bsz, t_len, _ = x.shape
    per = tm // HALO
    n_halo = t_len // HALO
    main = lambda col: pl.BlockSpec((1, tm, C_WIDTH), lambda b, i: (b, i, col))
    prev = lambda col: pl.BlockSpec((1, HALO, C_WIDTH), lambda b, i: (b, jnp.maximum(i * per - 1, 0), col))
    nxt = lambda col: pl.BlockSpec((1, HALO, C_WIDTH),
                                   lambda b, i: (b, jnp.minimum((i + 1) * per, n_halo - 1), col))
    xspec = pl.BlockSpec((1, tm, D_MODEL), lambda b, i: (b, i, 0))
    return pl.pallas_call(
        _odd_out_kernel,
        grid=(bsz, t_len // tm),
        in_specs=[main(0), main(1), main(2), main(3), prev(1), prev(2), nxt(1), nxt(2),
                  pl.BlockSpec(conv_w.shape, lambda b, i: (0, 0)),
                  pl.BlockSpec((1, C_WIDTH), lambda b, i: (0, 0)),
                  xspec,
                  pl.BlockSpec((1, 1, 3 * D_MODEL), lambda b, i: (mod_base + mod_stride * b, 0, 0)),
                  pl.BlockSpec((C_WIDTH, D_MODEL), lambda b, i: (0, 0))],
        out_specs=xspec,
        out_shape=jax.ShapeDtypeStruct(x.shape, F32),
        compiler_params=_params("parallel", "parallel"),
        name="odd_out",
    )(proj, proj, proj, proj, proj, proj, proj, proj, conv_w, conv_b, x, mods, w_out)


def _rope_tables(t_len):
    pos = jnp.arange(t_len, dtype=jnp.int32)
    row = (pos // GRID_W).astype(F32)
    col = (pos % GRID_W).astype(F32)
    half = B_HD // 2
    inv = ROPE_BASE ** (-jnp.arange(0, half, 2, dtype=F32) / half)
    ar = row[:, None] * inv
    ac = col[:, None] * inv
    cos = jnp.concatenate([jnp.cos(ar), jnp.cos(ar), jnp.cos(ac), jnp.cos(ac)], axis=-1)
    sin = jnp.concatenate([-jnp.sin(ar), jnp.sin(ar), -jnp.sin(ac), jnp.sin(ac)], axis=-1)
    return jnp.tile(cos, (1, 2)), jnp.tile(sin, (1, 2))


def _proj_rows(x, norm_w, mods, mod_base, mod_stride, w, flatten):
    bsz, t_len, _ = x.shape
    if flatten:
        x = x.reshape(1, bsz * t_len, D_MODEL)
    rows = x.shape[1]
    tm = next(c for c in (2048, 1024, 512, 256, 128) if rows % c == 0)
    out = _in_proj(x, norm_w, mods, mod_base, mod_stride, w, tm)
    return out.reshape(bsz, t_len, -1)


def _even_layer(x, mods, mod_base, mod_stride, flatten, norm_w, w_in, w_out, rw, dp, lam_init,
                s0, ctx_k, ctx_v, rope, emit_cache):
    bsz, t_len, _ = x.shape
    proj = _proj_rows(x, norm_w, mods, mod_base, mod_stride, w_in, flatten)
    mu_rkv, mu_lo, w0, w_up, a0, a_up, k_k, k_a, r_k, lnx_w, lnx_b = rw
    q_norm2, k_norm2, lam_vec, subln = dp
    tm = min(t_len, 256)
    cos, sin = rope if rope is not None else (None, None)
    prep = _qk_prep(proj, q_norm2, k_norm2, cos, sin, tm, emit_cache)
    qn, kn = prep[:2]
    yb = _attention(lam_vec, qn, kn, ctx_k, ctx_v, proj, subln, lam_init, min(t_len, 512))
    scan = lambda d: _rwkv(proj, s0[d], mu_rkv, mu_lo, w0[d:d + 1], w_up[d], a0[d:d + 1], a_up[d], k_k, k_a, r_k,
                           reverse=(d == 1))
    o_f, bon_f, s_f = scan(0)
    o_b, bon_b, s_b = scan(1)
    s_new = jnp.stack([s_f, s_b], axis=0)
    x_new = _even_out(o_f, o_b, bon_f, bon_b, proj, yb, x, mods, mod_base, mod_stride, lnx_w, lnx_b, w_out, tm)
    k_cache = prep[2] if emit_cache else None
    v_cache = proj[:, :, 6 * D_MODEL:7 * D_MODEL].astype(F32) if emit_cache else None
    return x_new, s_new, k_cache, v_cache


def _odd_layer(x, mods, mod_base, mod_stride, flatten, norm_w, w_in, conv_w, conv_b, w_out):
    proj = _proj_rows(x, norm_w, mods, mod_base, mod_stride, w_in, flatten)
    tm = min(x.shape[1], 256)
    return _odd_out(proj, x, mods, mod_base, mod_stride, conv_w, conv_b, w_out, tm)


def kernel(x_prompt, x_sample, state_rwkv_fwd, state_rwkv_bwd, cache_diff_k, cache_diff_v, c, c_ctx,
           norm_w, ada_w, ada_b,
           e_w_in, e_w_out, e_mu, e_w0, e_w_up, e_a0, e_a_up, e_k_k, e_k_a, e_r_k, e_lnx_w, e_lnx_b,
           e_q_norm, e_k_norm, e_lambda, e_subln,
           o_w_in, o_conv_w, o_conv_b, o_w_out):
    depth = norm_w.shape[0]
    bsz, seq = x_prompt.shape[:2]
    dec_b, dec_t = x_sample.shape[:2]
    past = cache_diff_k.shape[2]
    assert dec_b + 1 <= MOD_ROWS
    cvec = jnp.concatenate([c_ctx[None, :], c, jnp.zeros((MOD_ROWS - 1 - dec_b, D_MODEL), F32)], axis=0)
    mods = _ada(cvec, ada_w, ada_b)
    rope = _rope_tables(dec_t)
    rkv_w = 3 * A_WIDTH
    lora_end = rkv_w + 4 * LORA

    xp, xs = x_prompt, x_sample
    new_sf, new_sb, new_k, new_v = [], [], [], []
    for layer in range(depth):
        i = layer // 2
        nw = norm_w[layer][None, :]
        base = layer * MOD_ROWS
        if layer % 2 == 0:
            lam_init = 0.8 - 0.6 * math.exp(-0.3 * layer)
            w_in = jnp.concatenate([e_w_in[i][:, :rkv_w], e_w_in[i][:, lora_end:], e_w_in[i][:, rkv_w:lora_end]],
                                   axis=1).astype(BF16)
            w_out = e_w_out[i].astype(BF16)
            row = lambda t: t.reshape(1, -1)
            rw = (row(e_mu[i][:rkv_w]), row(e_mu[i][rkv_w:lora_end]), e_w0[i], e_w_up[i], e_a0[i], e_a_up[i],
                  row(e_k_k[i]), row(e_k_a[i]), row(e_r_k[i]), row(e_lnx_w[i]), row(e_lnx_b[i]))
            dp = (jnp.tile(e_q_norm[i], 2)[None, :], jnp.tile(e_k_norm[i], 2)[None, :], e_lambda[i],
                  row(e_subln[i]))
            zeros = jnp.zeros((2, bsz) + _STATE_BD, F32)
            xp, s_p, k_c, v_c = _even_layer(xp, mods, base, 0, True, nw, w_in, w_out, rw, dp, lam_init,
                                            zeros, None, None, None, True)
            s_p = _state_from_bd(s_p)
            s0 = _state_to_bd(jnp.stack([state_rwkv_fwd[:, i], state_rwkv_bwd[:, i]], axis=0))
            ctx_k = cache_diff_k[:, i].reshape(dec_b, past, B_WIDTH).astype(BF16)
            ctx_v = cache_diff_v[:, i].reshape(dec_b, past, B_WIDTH).astype(BF16)
            xs, _, _, _ = _even_layer(xs, mods, base + 1, 1, False, nw, w_in, w_out, rw, dp, lam_init,
                                      s0, ctx_k, ctx_v, rope, False)
            new_sf.append(s_p[0])
            new_sb.append(s_p[1])
            new_k.append(k_c.reshape(bsz, seq, B_HEADS, 2, B_HD))
            new_v.append(v_c.reshape(bsz, seq, B_HEADS, 2 * B_HD))
        else:
            w_in = o_w_in[i].astype(BF16)
            w_out = o_w_out[i].astype(BF16)
            cb = o_conv_b[i][None, :]
            xp = _odd_layer(xp, mods, base, 0, True, nw, w_in, o_conv_w[i], cb, w_out)
            xs = _odd_layer(xs, mods, base + 1, 1, False, nw, w_in, o_conv_w[i], cb, w_out)
    return (xp, xs, jnp.stack(new_sf, axis=1), jnp.stack(new_sb, axis=1),
            jnp.stack(new_k, axis=1), jnp.stack(new_v, axis=1))
```

```python
import functools
import math

import jax
import jax.numpy as jnp
from jax import lax
from jax.experimental import pallas as pl
from jax.experimental.pallas import tpu as pltpu

F32 = jnp.float32
BF16 = jnp.bfloat16
HI = lax.Precision.HIGHEST
PH = None

D_MODEL = 1024
A_WIDTH = 1024
A_HEAD = 64
A_HEADS = A_WIDTH // A_HEAD
LORA = 64
B_HD = 64
B_HEADS = 8
B_WIDTH = 1024
C_WIDTH = 2048
GRID_W = 64
ROPE_BASE = 10000.0
NORM_EPS = 1e-6
GN_EPS = 64e-5
MOD_ROWS = 8
CHUNK = 64
RWKV_SUB = 2
SUBLANES = 8
HALO = 16
VMEM_LIMIT = 48 * 1024 * 1024
ATTN_ROWS = 128

_NT = (((1,), (1,)), ((), ()))
_TN = (((0,), (0,)), ((), ()))


def _dot(x, y, dims=None, precision=None):
    if dims is None:
        return jnp.dot(x, y, precision=precision, preferred_element_type=F32)
    return lax.dot_general(x, y, dims, precision=precision, preferred_element_type=F32)


def _params(*sem):
    return pltpu.CompilerParams(dimension_semantics=sem, vmem_limit_bytes=VMEM_LIMIT)


def _ada_kernel(c_ref, w_ref, b_ref, o_ref):
    cv = c_ref[...]
    s = cv * jax.nn.sigmoid(cv)
    o_ref[0] = _dot(s, w_ref[0], precision=HI) + b_ref[0]


def _ada(cvec, ada_w, ada_b):
    depth = ada_w.shape[0]
    tn = D_MODEL
    out = pl.pallas_call(
        _ada_kernel,
        grid=(depth, 3 * D_MODEL // tn),
        in_specs=[pl.BlockSpec((MOD_ROWS, D_MODEL), lambda l, j: (0, 0)),
                  pl.BlockSpec((1, D_MODEL, tn), lambda l, j: (l, 0, j)),
                  pl.BlockSpec((1, 1, tn), lambda l, j: (l, 0, j))],
        out_specs=pl.BlockSpec((1, MOD_ROWS, tn), lambda l, j: (l, 0, j)),
        out_shape=jax.ShapeDtypeStruct((depth, MOD_ROWS, 3 * D_MODEL), F32),
        compiler_params=_params("parallel", "parallel"),
        name="ada",
    )(cvec, ada_w, ada_b.reshape(depth, 1, 3 * D_MODEL))
    return out.reshape(depth * MOD_ROWS, 1, 3 * D_MODEL)


def _in_proj_kernel(x_ref, nw_ref, mod_ref, w_ref, o_ref, h_ref):
    @pl.when(pl.program_id(2) == 0)
    def _():
        x = x_ref[0]
        ms = jnp.mean(x * x, axis=-1, keepdims=True)
        y = x * lax.rsqrt(ms + NORM_EPS) * nw_ref[...]
        m = mod_ref[0]
        h = y * (1.0 + m[:, D_MODEL:2 * D_MODEL]) + m[:, :D_MODEL]
        h_ref[...] = h.astype(BF16)

    o_ref[0] = _dot(h_ref[...], w_ref[...]).astype(o_ref.dtype)


def _in_proj(x, norm_w, mods, mod_base, mod_stride, w, tm):
    bsz, t_len, _ = x.shape
    n_out = w.shape[1]
    tn = next(c for c in (1024, 768, 512, 256, 128) if n_out % c == 0)
    return pl.pallas_call(
        _in_proj_kernel,
        grid=(bsz, t_len // tm, n_out // tn),
        in_specs=[pl.BlockSpec((1, tm, D_MODEL), lambda b, i, j: (b, i, 0)),
                  pl.BlockSpec((1, D_MODEL), lambda b, i, j: (0, 0)),
                  pl.BlockSpec((1, 1, 3 * D_MODEL), lambda b, i, j: (mod_base + mod_stride * b, 0, 0)),
                  pl.BlockSpec((D_MODEL, tn), lambda b, i, j: (0, j))],
        out_specs=pl.BlockSpec((1, tm, tn), lambda b, i, j: (b, i, j)),
        out_shape=jax.ShapeDtypeStruct((bsz, t_len, n_out), BF16),
        scratch_shapes=[pltpu.VMEM((tm, D_MODEL), BF16)],
        compiler_params=_params("parallel", "parallel", "arbitrary"),
        name="in_proj",
    )(x, norm_w, mods, w)


def _qk_prep_kernel(*refs, rope, emit_cache):
    pq_ref, pk_ref, qn_ref, kn_ref = refs[:4]
    pos = 4
    if rope:
        cos_ref, sin_ref = refs[pos:pos + 2]
        pos += 2
    q_out, k_out = refs[pos:pos + 2]
    kc_out = refs[pos + 2] if emit_cache else None
    width = 2 * B_HD
    tm = pq_ref.shape[1]
    lane = lax.broadcasted_iota(jnp.int32, (tm, width), 1)
    first16 = (lane % 32) < 16

    grp = lax.broadcasted_iota(jnp.int32, (width, width), 0) // B_HD
    same = jnp.where(grp == lax.broadcasted_iota(jnp.int32, (width, width), 1) // B_HD, 1.0, 0.0).astype(BF16)

    def norm(x, w):
        ss = x * x
        hi = ss.astype(BF16)
        lo = (ss - hi.astype(F32)).astype(BF16)
        ms = (_dot(hi, same) + _dot(lo, same)) * (1.0 / B_HD)
        return x * lax.rsqrt(ms + NORM_EPS) * w

    def rot(x):
        if not rope:
            return x
        partner = jnp.where(first16, pltpu.roll(x, width - 16, 1), pltpu.roll(x, 16, 1))
        return x * cos_ref[...] + partner * sin_ref[...]

    for h in range(B_HEADS):
        sl = slice(h * width, (h + 1) * width)
        q = norm(pq_ref[0, :, sl].astype(F32), qn_ref[...])
        k = norm(pk_ref[0, :, sl].astype(F32), kn_ref[...])
        if emit_cache:
            kc_out[0, :, sl] = k
        q_out[0, :, sl] = (rot(q) * (B_HD ** -0.5 * math.log2(math.e))).astype(BF16)
        k_out[0, :, sl] = rot(k).astype(BF16)


def _qk_prep(proj, q_norm2, k_norm2, cos, sin, tm, emit_cache):
    bsz, t_len, _ = proj.shape
    rope = cos is not None
    col = lambda c: pl.BlockSpec((1, tm, B_WIDTH), lambda b, i: (b, i, c))
    vec = pl.BlockSpec((1, 2 * B_HD), lambda b, i: (0, 0))
    in_specs = [col(4), col(5), vec, vec]
    args = [proj, proj, q_norm2, k_norm2]
    if rope:
        tab = pl.BlockSpec((tm, 2 * B_HD), lambda b, i: (i, 0))
        in_specs += [tab, tab]
        args += [cos, sin]
    out_spec = pl.BlockSpec((1, tm, B_WIDTH), lambda b, i: (b, i, 0))
    out_shape = [jax.ShapeDtypeStruct((bsz, t_len, B_WIDTH), BF16)] * 2
    out_specs = [out_spec] * 2
    if emit_cache:
        out_shape = out_shape + [jax.ShapeDtypeStruct((bsz, t_len, B_WIDTH), F32)]
        out_specs = out_specs + [out_spec]
    return pl.pallas_call(
        functools.partial(_qk_prep_kernel, rope=rope, emit_cache=emit_cache),
        grid=(bsz, t_len // tm),
        in_specs=in_specs, out_specs=out_specs, out_shape=out_shape,
        compiler_params=_params("parallel", "parallel"),
        name="qk_prep",
    )(*args)


def _attn_kernel(*refs, lam_init, has_ctx):
    lam_ref, q_ref, k_ref, v_ref = refs[:4]
    pos = 4
    if has_ctx:
        ck_ref, cv_ref = refs[pos:pos + 2]
        pos += 2
    gb_ref, sub_ref, o_ref, k_s, v_s = refs[pos:pos + 5]
    width = 2 * B_HD
    t_len = k_ref.shape[1]
    tq = q_ref.shape[1]

    @pl.when(pl.program_id(2) == 0)
    def _():
        k_s[0:t_len, :] = k_ref[0]
        v_s[0:t_len, 0:width] = v_ref[0]
        if has_ctx:
            k_s[t_len:, :] = ck_ref[0]
            v_s[t_len:, 0:width] = cv_ref[0]
        v_s[:, width:] = jnp.ones((v_s.shape[0], width), BF16)

    q = q_ref[0]
    lane = lax.broadcasted_iota(jnp.int32, q.shape, 1)
    zero = jnp.zeros_like(q)
    q2 = jnp.concatenate([jnp.where(lane < B_HD, q, zero), jnp.where(lane < B_HD, zero, q)], axis=0)
    n_groups = 2 * tq // ATTN_ROWS
    scores = lambda g: _dot(q2[g * ATTN_ROWS:(g + 1) * ATTN_ROWS], k_s[...], _NT)

    def softmax_pv(s):
        m = jnp.max(s, axis=-1, keepdims=True)
        acc = _dot(jnp.exp2(s - m).astype(BF16), v_s[...])
        return acc[:, :width] / acc[:, width:]

    parts = []
    s_next = scores(0)
    for g in range(n_groups):
        s_cur = s_next
        if g + 1 < n_groups:
            s_next = scores(g + 1)
        parts.append(softmax_pv(s_cur))
    comp = jnp.concatenate(parts, axis=0)

    lv = lam_ref[...]
    lam = (jnp.exp(jnp.sum(lv[0:1] * lv[1:2], axis=-1, keepdims=True))
           - jnp.exp(jnp.sum(lv[2:3] * lv[3:4], axis=-1, keepdims=True)) + lam_init)
    o = comp[:tq] - lam * comp[tq:]
    ms = jnp.mean(o * o, axis=-1, keepdims=True)
    o = o * lax.rsqrt(ms + NORM_EPS) * sub_ref[...] * (1.0 - lam_init)
    gb = gb_ref[0].astype(F32)
    o_ref[0] = (o * (gb * jax.nn.sigmoid(gb))).astype(BF16)


def _attention(lam_vec, q, k, ctx_k, ctx_v, proj, subln, lam_init, tq):
    bsz, t_len, _ = q.shape
    has_ctx = ctx_k is not None
    width = 2 * B_HD
    qspec = pl.BlockSpec((1, tq, width), lambda b, h, i: (b, i, h))
    kspec = pl.BlockSpec((1, t_len, width), lambda b, h, i: (b, 0, h))
    vspec = pl.BlockSpec((1, t_len, width), lambda b, h, i: (b, 0, 6 * B_HEADS + h))
    in_specs = [pl.BlockSpec((4, B_HD), lambda b, h, i: (0, 0)), qspec, kspec, vspec]
    args = [lam_vec, q, k, proj]
    if has_ctx:
        cspec = pl.BlockSpec((1, ctx_k.shape[1], width), lambda b, h, i: (b, 0, h))
        in_specs += [cspec, cspec]
        args += [ctx_k, ctx_v]
    in_specs += [pl.BlockSpec((1, tq, width), lambda b, h, i: (b, i, 7 * B_HEADS + h)),
                 pl.BlockSpec((1, width), lambda b, h, i: (0, 0))]
    args += [proj, subln]
    n_keys = t_len + (ctx_k.shape[1] if has_ctx else 0)
    return pl.pallas_call(
        functools.partial(_attn_kernel, lam_init=lam_init, has_ctx=has_ctx),
        grid=(bsz, B_HEADS, t_len // tq),
        in_specs=in_specs, out_specs=qspec,
        out_shape=jax.ShapeDtypeStruct((bsz, t_len, B_WIDTH), BF16),
        scratch_shapes=[pltpu.VMEM((n_keys, width), BF16), pltpu.VMEM((n_keys, 2 * width), BF16)],
        compiler_params=_params("parallel", "parallel", "arbitrary"),
        name="diff_attn",
    )(*args)


def _rwkv_kernel(rkv_ref, rkv_p_ref, rkv_n_ref, lo_ref, lo_p_ref, lo_n_ref, mu_rkv_ref, mu_lo_ref,
                 w0_ref, wup_ref, a0_ref, aup_ref, kk_ref, ka_ref, rk_ref, s0_ref,
                 o_ref, bon_ref, s_ref,
                 xpad_s, st_s, *, n_steps, n_sub, reverse):
    C = CHUNK
    R = n_sub * C
    PAD = SUBLANES
    c = pl.program_id(1)
    blk = n_steps - 1 - c if reverse else c
    first = blk == 0
    last = blk == n_steps - 1
    row = lax.broadcasted_iota(jnp.int32, (R, 1), 0)
    up = lambda ref: ref[0].astype(F32)

    block = {}

    def block_values():
        if block:
            return block["logw"], block["a"]
        lo = up(lo_ref)
        lo_prev = jnp.where(first, 0.0, up(lo_p_ref)[HALO - 1:HALO, :])
        lo_next = jnp.where(last, 0.0, up(lo_n_ref)[0:1, :])
        lo_p = jnp.where(row == 0, lo_prev, pltpu.roll(lo, 1, 0))
        lo_n = jnp.where(row == R - 1, lo_next, pltpu.roll(lo, R - 1, 0))
        lo = lo + mu_lo_ref[...] * (0.5 * (lo_p + lo_n) - lo)
        dcol = 1 if reverse else 0
        wl = lo[:, dcol * LORA:(dcol + 1) * LORA]
        al = lo[:, (2 + dcol) * LORA:(3 + dcol) * LORA]
        xw = w0_ref[...] + _dot(jnp.tanh(wl), wup_ref[...], precision=PH)
        block["logw"] = -math.exp(-0.5) * jax.nn.sigmoid(xw)
        block["a"] = jax.nn.sigmoid(a0_ref[...] + _dot(al, aup_ref[...], precision=PH))
        xpad_s[PAD:PAD + R, :] = up(rkv_ref)
        xpad_s[PAD - 1:PAD, :] = jnp.where(first, 0.0, up(rkv_p_ref)[HALO - 1:HALO, :])
        xpad_s[PAD + R:PAD + R + 1, :] = jnp.where(last, 0.0, up(rkv_n_ref)[0:1, :])
        return block["logw"], block["a"]

    ti = lax.broadcasted_iota(jnp.int32, (C, C), 0)
    si = lax.broadcasted_iota(jnp.int32, (C, C), 1)
    incl = (si >= ti) if reverse else (si <= ti)
    tri = jnp.where(incl, 1.0, 0.0).astype(BF16)

    @pl.when(c == 0)
    def _():
        st_s[...] = s0_ref[0]

    W2 = 2 * A_HEAD
    lane = lax.broadcasted_iota(jnp.int32, (C, W2), 1)
    trow = lax.broadcasted_iota(jnp.int32, (C, W2), 0)
    low = lane < A_HEAD
    before2 = (lane % A_HEAD - trow) if reverse else (trow - lane % A_HEAD)
    strict2 = before2 > 0
    incl2 = before2 >= 0
    eye2 = jnp.where(before2 == 0, 1.0, 0.0)
    brow = lax.broadcasted_iota(jnp.int32, (W2, W2), 0)
    bcol = lax.broadcasted_iota(jnp.int32, (W2, W2), 1)
    bdmask = (brow < A_HEAD) == (bcol < A_HEAD)
    bdmask2 = jnp.concatenate([bdmask, bdmask], axis=1)
    eye_bd = jnp.where(brow == bcol, 1.0, 0.0)

    def bd(x):
        return jnp.where(bdmask if x.shape[1] == W2 else bdmask2, jnp.concatenate([x, x], axis=0), 0.0)

    H = C // 2
    lane_h = lax.broadcasted_iota(jnp.int32, (H, W2), 1)
    near = (lane_h % A_HEAD) < H
    eye4 = jnp.where(lax.broadcasted_iota(jnp.int32, (H, W2), 0) == lane_h % H, 1.0, 0.0)
    mask4 = brow // H == bcol // H
    shift4 = (brow // H + 1 == bcol // H) if reverse else (brow // H == bcol // H + 1)

    def bd4(x, mask=mask4):
        return jnp.where(mask, jnp.concatenate([x] * 4, axis=0), 0.0)

    def head_sum(x):
        s0 = jnp.sum(jnp.where(low, x, 0.0), axis=-1, keepdims=True)
        s1 = jnp.sum(jnp.where(low, 0.0, x), axis=-1, keepdims=True)
        return jnp.where(low, s0, s1)

    def rows(*xs):
        return jnp.concatenate(xs, axis=0)

    n_pairs = A_HEADS // 2
    state = [st_s[p] for p in range(n_pairs)]

    def prepare(j, p):
        rs = slice(j * C, (j + 1) * C)
        sl = slice(p * W2, (p + 1) * W2)
        logw, a = block_values()

        def shifted(col0):
            cs = slice(col0 + p * W2, col0 + (p + 1) * W2)
            base = PAD + j * C
            x = xpad_s[base:base + C, cs]
            nb = xpad_s[base - 1:base - 1 + C, cs] + xpad_s[base + 1:base + 1 + C, cs]
            return x + mu_rkv_ref[:, cs] * (0.5 * nb - x)

        r = shifted(0)
        k = shifted(A_WIDTH)
        vm = shifted(2 * A_WIDTH)
        lw = logw[rs, sl]
        av = a[rs, sl]
        kd = k * (1.0 + (av - 1.0) * ka_ref[:, sl])
        kk = k * kk_ref[:, sl]
        kk = kk / jnp.maximum(jnp.sqrt(head_sum(kk * kk)), 1e-12)
        bon_ref[0, rs, sl] = head_sum(r * kd * rk_ref[:, sl]) * vm
        t0 = lw.astype(BF16)
        e0 = lw - t0.astype(F32)
        t1 = e0.astype(BF16)
        t2 = (e0 - t1.astype(F32)).astype(BF16)
        cum = _dot(tri, t0) + _dot(tri, t1) + _dot(tri, t2)
        pinv = jnp.exp(-cum)
        pt = jnp.exp(jnp.sum(lw, axis=0, keepdims=True))
        return kk * jnp.exp(cum - lw), kk * av * pinv, kd * pinv, r * jnp.exp(cum), vm, pt

    def solve(j, p, am, bm, kp, rm, vm, pt):
        rs = slice(j * C, (j + 1) * C)
        sl = slice(p * W2, (p + 1) * W2)
        ar = rows(am, rm)
        xb = _dot(ar, bd(bm), _NT, PH)
        xk = _dot(ar, bd(kp), _NT, PH)
        yield
        nab = jnp.where(strict2, -xb[:C], 0.0)
        mak = jnp.where(strict2, xk[:C], 0.0)
        mrb = jnp.where(incl2, xb[C:], 0.0)
        mrk = jnp.where(incl2, xk[C:], 0.0)
        np4 = jnp.where(near, nab[:H], nab[H:])
        t4 = eye4 + np4
        pw = _dot(np4, bd4(np4), precision=PH)
        mv = _dot(rows(mak, mrk), bd(vm), precision=PH)
        yield
        for _ in range(int(math.log2(H)) - 2):
            both = _dot(rows(t4, pw), bd4(pw), precision=PH)
            t4 = t4 + both[:H]
            pw = both[H:]
            yield
        t4 = t4 + _dot(t4, bd4(pw), precision=PH)
        yield
        n_off = jnp.where(near, 0.0, nab[:H]) if reverse else jnp.where(near, nab[H:], 0.0)
        x_off = _dot(n_off, bd4(t4), precision=PH)
        yield
        t_off = _dot(t4, bd4(x_off, shift4), precision=PH)
        if reverse:
            tinv = rows(jnp.where(near, t4, t_off), jnp.where(near, 0.0, t4))
        else:
            tinv = rows(jnp.where(near, t4, 0.0), jnp.where(near, t_off, t4))
        yield
        wu = jnp.concatenate([_dot(tinv, bd(am), precision=PH), _dot(tinv, bd(mv[:C]), precision=PH)], axis=1)
        yield
        mw = _dot(mrb, bd(wu), precision=PH)
        bwu = _dot(bm * pt, wu, _TN, PH)
        kv = _dot(kp * pt, vm, _TN, PH)
        yield
        rp = rm - mw[:, :W2]
        o0 = mv[C:] - mw[:, W2:]
        gt = eye_bd * pt - jnp.where(bdmask, bwu[:, :W2], 0.0)
        ht = jnp.where(bdmask, kv - bwu[:, W2:], 0.0)
        oc = _dot(rows(rp, gt), state[p], precision=PH)
        o_ref[0, rs, sl] = oc[:C] + o0
        state[p] = oc[C:] + ht

    order = range(n_sub - 1, -1, -1) if reverse else range(n_sub)
    items = [(j, p) for j in order for p in range(n_pairs)]
    operands = [prepare(j, p) for j, p in items]
    active = [solve(j, p, *ops) for (j, p), ops in zip(items, operands)]
    while active:
        active = [gen for gen in active if next(gen, "done") != "done"]
    for p in range(n_pairs):
        st_s[p] = state[p]

    @pl.when(c == n_steps - 1)
    def _():
        s_ref[0] = st_s[...]


_STATE_BD = (A_HEADS // 2, 2 * A_HEAD, 2 * A_HEAD)


def _state_to_bd(s):
    lead = s.shape[:-3]
    st = jnp.swapaxes(s, -1, -2).reshape(lead + (A_HEADS // 2, 2, A_HEAD, A_HEAD))
    z = jnp.zeros_like(st[..., 0, :, :])
    top = jnp.concatenate([st[..., 0, :, :], z], axis=-1)
    bot = jnp.concatenate([z, st[..., 1, :, :]], axis=-1)
    return jnp.concatenate([top, bot], axis=-2)


def _state_from_bd(st):
    pair = jnp.stack([st[..., :A_HEAD, :A_HEAD], st[..., A_HEAD:, A_HEAD:]], axis=-3)
    return jnp.swapaxes(pair.reshape(st.shape[:-3] + (A_HEADS, A_HEAD, A_HEAD)), -1, -2)


def _rwkv(proj, s0, mu_rkv, mu_lo, w0, w_up, a0, a_up, k_k, k_a, r_k, reverse):
    bsz, t_len, _ = proj.shape
    n_sub = next(n for n in (RWKV_SUB, 2, 1) if t_len % (n * CHUNK) == 0)
    rows = n_sub * CHUNK
    n_steps = t_len // rows
    n_halo = t_len // HALO
    per = rows // HALO
    blk = (lambda c: n_steps - 1 - c) if reverse else (lambda c: c)
    rkv_w = 3 * A_WIDTH
    lo_w = 4 * LORA
    lo_col = (8 * D_MODEL) // lo_w

    def main(width, col):
        return pl.BlockSpec((1, rows, width), lambda b, c: (b, blk(c), col))

    def prev(width, col):
        return pl.BlockSpec((1, HALO, width), lambda b, c: (b, jnp.maximum(blk(c) * per - 1, 0), col))

    def nxt(width, col):
        return pl.BlockSpec((1, HALO, width), lambda b, c: (b, jnp.minimum((blk(c) + 1) * per, n_halo - 1), col))

    vec = lambda width: pl.BlockSpec((1, width), lambda b, c: (0, 0))
    mat = pl.BlockSpec((LORA, A_WIDTH), lambda b, c: (0, 0))
    sspec = pl.BlockSpec((1,) + _STATE_BD, lambda b, c: (b, 0, 0, 0))
    ospec = pl.BlockSpec((1, rows, A_WIDTH), lambda b, c: (b, blk(c), 0))
    big = jax.ShapeDtypeStruct((bsz, t_len, A_WIDTH), F32)
    return pl.pallas_call(
        functools.partial(_rwkv_kernel, n_steps=n_steps, n_sub=n_sub, reverse=reverse),
        grid=(bsz, n_steps),
        in_specs=[main(rkv_w, 0), prev(rkv_w, 0), nxt(rkv_w, 0),
                  main(lo_w, lo_col), prev(lo_w, lo_col), nxt(lo_w, lo_col),
                  vec(rkv_w), vec(lo_w), vec(A_WIDTH), mat, vec(A_WIDTH), mat,
                  vec(A_WIDTH), vec(A_WIDTH), vec(A_WIDTH), sspec],
        out_specs=[ospec, ospec, sspec],
        out_shape=[big, big, jax.ShapeDtypeStruct((bsz,) + _STATE_BD, F32)],
        scratch_shapes=[pltpu.VMEM((rows + 2 * SUBLANES, rkv_w), F32), pltpu.VMEM(_STATE_BD, F32)],
        compiler_params=_params("parallel", "arbitrary"),
        name="rwkv_bwd" if reverse else "rwkv_fwd",
    )(proj, proj, proj, proj, proj, proj, mu_rkv, mu_lo, w0, w_up, a0, a_up, k_k, k_a, r_k, s0)


def _even_out_kernel(of_ref, ob_ref, bf_ref, bb_ref, ga_ref, yb_ref, x_ref, mod_ref, lw_ref, lb_ref, w_ref,
                     o_ref, y_s):
    o = of_ref[0] + ob_ref[0]
    bon = bf_ref[0] + bb_ref[0]
    ga = ga_ref[0].astype(F32)
    gate_a = ga * jax.nn.sigmoid(ga)
    W2 = 2 * A_HEAD
    low = lax.broadcasted_iota(jnp.int32, (o.shape[0], W2), 1) < A_HEAD

    def head_mean(x):
        s0 = jnp.sum(jnp.where(low, x, 0.0), axis=-1, keepdims=True)
        s1 = jnp.sum(jnp.where(low, 0.0, x), axis=-1, keepdims=True)
        return jnp.where(low, s0, s1) * (1.0 / A_HEAD)

    sls = [slice(p * W2, (p + 1) * W2) for p in range(A_HEADS // 2)]
    cen = [o[:, sl] for sl in sls]
    cen = [x - head_mean(x) for x in cen]
    var = [head_mean(x * x) for x in cen]
    for sl, x, vr in zip(sls, cen, var):
        yh = x * lax.rsqrt(vr + GN_EPS) * lw_ref[:, sl] + lb_ref[:, sl]
        y_s[:, sl] = ((yh + bon[:, sl]) * gate_a[:, sl]).astype(BF16)
    y_s[:, A_WIDTH:] = yb_ref[0]
    out = _dot(y_s[...], w_ref[...])
    gate = mod_ref[0][:, 2 * D_MODEL:]
    o_ref[0] = x_ref[0] + gate * out


def _even_out(o_f, o_b, bon_f, bon_b, proj, yb, x, mods, mod_base, mod_stride, lnx_w, lnx_b, w_out, tm):
    bsz, t_len, _ = x.shape
    row = lambda col: pl.BlockSpec((1, tm, D_MODEL), lambda b, i: (b, i, col))
    vec = pl.BlockSpec((1, A_WIDTH), lambda b, i: (0, 0))
    return pl.pallas_call(
        _even_out_kernel,
        grid=(bsz, t_len // tm),
        in_specs=[row(0), row(0), row(0), row(0), row(3), row(0), row(0),
                  pl.BlockSpec((1, 1, 3 * D_MODEL), lambda b, i: (mod_base + mod_stride * b, 0, 0)),
                  vec, vec, pl.BlockSpec((A_WIDTH + B_WIDTH, D_MODEL), lambda b, i: (0, 0))],
        out_specs=row(0),
        out_shape=jax.ShapeDtypeStruct(x.shape, F32),
        scratch_shapes=[pltpu.VMEM((tm, A_WIDTH + B_WIDTH), BF16)],
        compiler_params=_params("parallel", "parallel"),
        name="even_out",
    )(o_f, o_b, bon_f, bon_b, proj, yb, x, mods, lnx_w, lnx_b, w_out)


def _odd_out_kernel(bg_ref, cg_ref, u_ref, z_ref, cgp_ref, up_ref, cgn_ref, un_ref, cw_ref, cb_ref,
                    x_ref, mod_ref, w_ref, o_ref):
    i = pl.program_id(1)
    tm = bg_ref.shape[1]
    row = lax.broadcasted_iota(jnp.int32, (tm, 1), 0)
    f32 = lambda ref: ref[0].astype(F32)
    cu = f32(cg_ref) * f32(u_ref)
    prev_row = jnp.where(i == 0, 0.0, (f32(cgp_ref) * f32(up_ref))[HALO - 1:HALO, :])
    next_row = jnp.where(i == pl.num_programs(1) - 1, 0.0, (f32(cgn_ref) * f32(un_ref))[0:1, :])
    cu_prev = jnp.where(row == 0, prev_row, pltpu.roll(cu, 1, 0))
    cu_next = jnp.where(row == tm - 1, next_row, pltpu.roll(cu, tm - 1, 0))
    cw = cw_ref[...]
    conv = cu_prev * cw[0:1] + cu * cw[1:2] + cu_next * cw[2:3] + cb_ref[...]
    z = f32(z_ref)
    y = f32(bg_ref) * conv * (z * jax.nn.sigmoid(z))
    out = _dot(y.astype(BF16), w_ref[...])
    gate = mod_ref[0][:, 2 * D_MODEL:]
    o_ref[0] = x_ref[0] + gate * out


def _odd_out(proj, x, mods, mod_base, mod_stride, conv_w, conv_b, w_out, tm):
    bsz, t_len, _ = x.shape
    per = tm // HALO
    n_halo = t_len // HALO
    main = lambda col: pl.BlockSpec((1, tm, C_WIDTH), lambda b, i: (b, i, col))
    prev = lambda col: pl.BlockSpec((1, HALO, C_WIDTH), lambda b, i: (b, jnp.maximum(i * per - 1, 0), col))
    nxt = lambda col: pl.BlockSpec((1, HALO, C_WIDTH),
                                   lambda b, i: (b, jnp.minimum((i + 1) * per, n_halo - 1), col))
    xspec = pl.BlockSpec((1, tm, D_MODEL), lambda b, i: (b, i, 0))
    return pl.pallas_call(
        _odd_out_kernel,
        grid=(bsz, t_len // tm),
        in_specs=[main(0), main(1), main(2), main(3), prev(1), prev(2), nxt(1), nxt(2),
                  pl.BlockSpec(conv_w.shape, lambda b, i: (0, 0)),
                  pl.BlockSpec((1, C_WIDTH), lambda b, i: (0, 0)),
                  xspec,
                  pl.BlockSpec((1, 1, 3 * D_MODEL), lambda b, i: (mod_base + mod_stride * b, 0, 0)),
                  pl.BlockSpec((C_WIDTH, D_MODEL), lambda b, i: (0, 0))],
        out_specs=xspec,
        out_shape=jax.ShapeDtypeStruct(x.shape, F32),
        compiler_params=_params("parallel", "parallel"),
        name="odd_out",
    )(proj, proj, proj, proj, proj, proj, proj, proj, conv_w, conv_b, x, mods, w_out)


def _rope_tables(t_len):
    pos = jnp.arange(t_len, dtype=jnp.int32)
    row = (pos // GRID_W).astype(F32)
    col = (pos % GRID_W).astype(F32)
    half = B_HD // 2
    inv = ROPE_BASE ** (-jnp.arange(0, half, 2, dtype=F32) / half)
    ar = row[:, None] * inv
    ac = col[:, None] * inv
    cos = jnp.concatenate([jnp.cos(ar), jnp.cos(ar), jnp.cos(ac), jnp.cos(ac)], axis=-1)
    sin = jnp.concatenate([-jnp.sin(ar), jnp.sin(ar), -jnp.sin(ac), jnp.sin(ac)], axis=-1)
    return jnp.tile(cos, (1, 2)), jnp.tile(sin, (1, 2))


def _proj_rows(x, norm_w, mods, mod_base, mod_stride, w, flatten):
    bsz, t_len, _ = x.shape
    if flatten:
        x = x.reshape(1, bsz * t_len, D_MODEL)
    rows = x.shape[1]
    tm = next(c for c in (2048, 1024, 512, 256, 128) if rows % c == 0)
    out = _in_proj(x, norm_w, mods, mod_base, mod_stride, w, tm)
    return out.reshape(bsz, t_len, -1)


def _even_layer(x, mods, mod_base, mod_stride, flatten, norm_w, w_in, w_out, rw, dp, lam_init,
                s0, ctx_k, ctx_v, rope, emit_cache):
    bsz, t_len, _ = x.shape
    proj = _proj_rows(x, norm_w, mods, mod_base, mod_stride, w_in, flatten)
    mu_rkv, mu_lo, w0, w_up, a0, a_up, k_k, k_a, r_k, lnx_w, lnx_b = rw
    q_norm2, k_norm2, lam_vec, subln = dp
    tm = min(t_len, 256)
    cos, sin = rope if rope is not None else (None, None)
    prep = _qk_prep(proj, q_norm2, k_norm2, cos, sin, tm, emit_cache)
    qn, kn = prep[:2]
    yb = _attention(lam_vec, qn, kn, ctx_k, ctx_v, proj, subln, lam_init, min(t_len, 1024))
    scan = lambda d: _rwkv(proj, s0[d], mu_rkv, mu_lo, w0[d:d + 1], w_up[d], a0[d:d + 1], a_up[d], k_k, k_a, r_k,
                           reverse=(d == 1))
    o_f, bon_f, s_f = scan(0)
    o_b, bon_b, s_b = scan(1)
    s_new = jnp.stack([s_f, s_b], axis=0)
    x_new = _even_out(o_f, o_b, bon_f, bon_b, proj, yb, x, mods, mod_base, mod_stride, lnx_w, lnx_b, w_out, tm)
    k_cache = prep[2] if emit_cache else None
    v_cache = proj[:, :, 6 * D_MODEL:7 * D_MODEL].astype(F32) if emit_cache else None
    return x_new, s_new, k_cache, v_cache


def _odd_layer(x, mods, mod_base, mod_stride, flatten, norm_w, w_in, conv_w, conv_b, w_out):
    proj = _proj_rows(x, norm_w, mods, mod_base, mod_stride, w_in, flatten)
    tm = min(x.shape[1], 256)
    return _odd_out(proj, x, mods, mod_base, mod_stride, conv_w, conv_b, w_out, tm)


def kernel(x_prompt, x_sample, state_rwkv_fwd, state_rwkv_bwd, cache_diff_k, cache_diff_v, c, c_ctx,
           norm_w, ada_w, ada_b,
           e_w_in, e_w_out, e_mu, e_w0, e_w_up, e_a0, e_a_up, e_k_k, e_k_a, e_r_k, e_lnx_w, e_lnx_b,
           e_q_norm, e_k_norm, e_lambda, e_subln,
           o_w_in, o_conv_w, o_conv_b, o_w_out):
    depth = norm_w.shape[0]
    bsz, seq = x_prompt.shape[:2]
    dec_b, dec_t = x_sample.shape[:2]
    past = cache_diff_k.shape[2]
    assert dec_b + 1 <= MOD_ROWS
    cvec = jnp.concatenate([c_ctx[None, :], c, jnp.zeros((MOD_ROWS - 1 - dec_b, D_MODEL), F32)], axis=0)
    mods = _ada(cvec, ada_w, ada_b)
    rope = _rope_tables(dec_t)
    rkv_w = 3 * A_WIDTH
    lora_end = rkv_w + 4 * LORA

    xp, xs = x_prompt, x_sample
    new_sf, new_sb, new_k, new_v = [], [], [], []
    for layer in range(depth):
        i = layer // 2
        nw = norm_w[layer][None, :]
        base = layer * MOD_ROWS
        if layer % 2 == 0:
            lam_init = 0.8 - 0.6 * math.exp(-0.3 * layer)
            w_in = jnp.concatenate([e_w_in[i][:, :rkv_w], e_w_in[i][:, lora_end:], e_w_in[i][:, rkv_w:lora_end]],
                                   axis=1).astype(BF16)
            w_out = e_w_out[i].astype(BF16)
            row = lambda t: t.reshape(1, -1)
            rw = (row(e_mu[i][:rkv_w]), row(e_mu[i][rkv_w:lora_end]), e_w0[i], e_w_up[i], e_a0[i], e_a_up[i],
                  row(e_k_k[i]), row(e_k_a[i]), row(e_r_k[i]), row(e_lnx_w[i]), row(e_lnx_b[i]))
            dp = (jnp.tile(e_q_norm[i], 2)[None, :], jnp.tile(e_k_norm[i], 2)[None, :], e_lambda[i],
                  row(e_subln[i]))
            zeros = jnp.zeros((2, bsz) + _STATE_BD, F32)
            xp, s_p, k_c, v_c = _even_layer(xp, mods, base, 0, True, nw, w_in, w_out, rw, dp, lam_init,
                                            zeros, None, None, None, True)
            s_p = _state_from_bd(s_p)
            s0 = _state_to_bd(jnp.stack([state_rwkv_fwd[:, i], state_rwkv_bwd[:, i]], axis=0))
            ctx_k = cache_diff_k[:, i].reshape(dec_b, past, B_WIDTH).astype(BF16)
            ctx_v = cache_diff_v[:, i].reshape(dec_b, past, B_WIDTH).astype(BF16)
            xs, _, _, _ = _even_layer(xs, mods, base + 1, 1, False, nw, w_in, w_out, rw, dp, lam_init,
                                      s0, ctx_k, ctx_v, rope, False)
            new_sf.append(s_p[0])
            new_sb.append(s_p[1])
            new_k.append(k_c.reshape(bsz, seq, B_HEADS, 2, B_HD))
            new_v.append(v_c.reshape(bsz, seq, B_HEADS, 2 * B_HD))
        else:
            w_in = o_w_in[i].astype(BF16)
            w_out = o_w_out[i].astype(BF16)
            cb = o_conv_b[i][None, :]
            xp = _odd_layer(xp, mods, base, 0, True, nw, w_in, o_conv_w[i], cb, w_out)
            xs = _odd_layer(xs, mods, base + 1, 1, False, nw, w_in, o_conv_w[i], cb, w_out)
    return (xp, xs, jnp.stack(new_sf, axis=1), jnp.stack(new_sb, axis=1),
            jnp.stack(new_k, axis=1), jnp.stack(new_v, axis=1))
```

```python
import functools
import math

import jax
import jax.numpy as jnp
from jax import lax
from jax.experimental import pallas as pl
from jax.experimental.pallas import tpu as pltpu

F32 = jnp.float32
BF16 = jnp.bfloat16
HI = lax.Precision.HIGHEST
PH = None

D_MODEL = 1024
A_WIDTH = 1024
A_HEAD = 64
A_HEADS = A_WIDTH // A_HEAD
LORA = 64
B_HD = 64
B_HEADS = 8
B_WIDTH = 1024
C_WIDTH = 2048
GRID_W = 64
ROPE_BASE = 10000.0
NORM_EPS = 1e-6
GN_EPS = 64e-5
MOD_ROWS = 8
CHUNK = 64
RWKV_SUB = 2
SUBLANES = 8
HALO = 16
VMEM_LIMIT = 48 * 1024 * 1024
ATTN_ROWS = 128

_NT = (((1,), (1,)), ((), ()))
_TN = (((0,), (0,)), ((), ()))


def _dot(x, y, dims=None, precision=None):
    if dims is None:
        return jnp.dot(x, y, precision=precision, preferred_element_type=F32)
    return lax.dot_general(x, y, dims, precision=precision, preferred_element_type=F32)


def _params(*sem):
    return pltpu.CompilerParams(dimension_semantics=sem, vmem_limit_bytes=VMEM_LIMIT)


def _ada_kernel(c_ref, w_ref, b_ref, o_ref):
    cv = c_ref[...]
    s = cv * jax.nn.sigmoid(cv)
    o_ref[0] = _dot(s, w_ref[0], precision=HI) + b_ref[0]


def _ada(cvec, ada_w, ada_b):
    depth = ada_w.shape[0]
    tn = D_MODEL
    out = pl.pallas_call(
        _ada_kernel,
        grid=(depth, 3 * D_MODEL // tn),
        in_specs=[pl.BlockSpec((MOD_ROWS, D_MODEL), lambda l, j: (0, 0)),
                  pl.BlockSpec((1, D_MODEL, tn), lambda l, j: (l, 0, j)),
                  pl.BlockSpec((1, 1, tn), lambda l, j: (l, 0, j))],
        out_specs=pl.BlockSpec((1, MOD_ROWS, tn), lambda l, j: (l, 0, j)),
        out_shape=jax.ShapeDtypeStruct((depth, MOD_ROWS, 3 * D_MODEL), F32),
        compiler_params=_params("parallel", "parallel"),
        name="ada",
    )(cvec, ada_w, ada_b.reshape(depth, 1, 3 * D_MODEL))
    return out.reshape(depth * MOD_ROWS, 1, 3 * D_MODEL)


def _in_proj_kernel(x_ref, nw_ref, mod_ref, w_ref, o_ref, h_ref):
    @pl.when(pl.program_id(2) == 0)
    def _():
        x = x_ref[0]
        ms = jnp.mean(x * x, axis=-1, keepdims=True)
        y = x * lax.rsqrt(ms + NORM_EPS) * nw_ref[...]
        m = mod_ref[0]
        h = y * (1.0 + m[:, D_MODEL:2 * D_MODEL]) + m[:, :D_MODEL]
        h_ref[...] = h.astype(BF16)

    o_ref[0] = _dot(h_ref[...], w_ref[...]).astype(o_ref.dtype)


def _in_proj(x, norm_w, mods, mod_base, mod_stride, w, tm):
    bsz, t_len, _ = x.shape
    n_out = w.shape[1]
    tn = next(c for c in (1024, 768, 512, 256, 128) if n_out % c == 0)
    return pl.pallas_call(
        _in_proj_kernel,
        grid=(bsz, t_len // tm, n_out // tn),
        in_specs=[pl.BlockSpec((1, tm, D_MODEL), lambda b, i, j: (b, i, 0)),
                  pl.BlockSpec((1, D_MODEL), lambda b, i, j: (0, 0)),
                  pl.BlockSpec((1, 1, 3 * D_MODEL), lambda b, i, j: (mod_base + mod_stride * b, 0, 0)),
                  pl.BlockSpec((D_MODEL, tn), lambda b, i, j: (0, j))],
        out_specs=pl.BlockSpec((1, tm, tn), lambda b, i, j: (b, i, j)),
        out_shape=jax.ShapeDtypeStruct((bsz, t_len, n_out), BF16),
        scratch_shapes=[pltpu.VMEM((tm, D_MODEL), BF16)],
        compiler_params=_params("parallel", "parallel", "arbitrary"),
        name="in_proj",
    )(x, norm_w, mods, w)


def _qk_prep_kernel(*refs, rope, emit_cache):
    pq_ref, pk_ref, qn_ref, kn_ref = refs[:4]
    pos = 4
    if rope:
        cos_ref, sin_ref = refs[pos:pos + 2]
        pos += 2
    q_out, k_out = refs[pos:pos + 2]
    kc_out = refs[pos + 2] if emit_cache else None
    width = 2 * B_HD
    tm = pq_ref.shape[1]
    lane = lax.broadcasted_iota(jnp.int32, (tm, width), 1)
    first16 = (lane % 32) < 16

    grp = lax.broadcasted_iota(jnp.int32, (width, width), 0) // B_HD
    same = jnp.where(grp == lax.broadcasted_iota(jnp.int32, (width, width), 1) // B_HD, 1.0, 0.0).astype(BF16)

    def norm(x, w):
        ss = x * x
        hi = ss.astype(BF16)
        lo = (ss - hi.astype(F32)).astype(BF16)
        ms = (_dot(hi, same) + _dot(lo, same)) * (1.0 / B_HD)
        return x * lax.rsqrt(ms + NORM_EPS) * w

    def rot(x):
        if not rope:
            return x
        partner = jnp.where(first16, pltpu.roll(x, width - 16, 1), pltpu.roll(x, 16, 1))
        return x * cos_ref[...] + partner * sin_ref[...]

    for h in range(B_HEADS):
        sl = slice(h * width, (h + 1) * width)
        q = norm(pq_ref[0, :, sl].astype(F32), qn_ref[...])
        k = norm(pk_ref[0, :, sl].astype(F32), kn_ref[...])
        if emit_cache:
            kc_out[0, :, sl] = k
        q_out[0, :, sl] = (rot(q) * (B_HD ** -0.5 * math.log2(math.e))).astype(BF16)
        k_out[0, :, sl] = rot(k).astype(BF16)


def _qk_prep(proj, q_norm2, k_norm2, cos, sin, tm, emit_cache):
    bsz, t_len, _ = proj.shape
    rope = cos is not None
    col = lambda c: pl.BlockSpec((1, tm, B_WIDTH), lambda b, i: (b, i, c))
    vec = pl.BlockSpec((1, 2 * B_HD), lambda b, i: (0, 0))
    in_specs = [col(4), col(5), vec, vec]
    args = [proj, proj, q_norm2, k_norm2]
    if rope:
        tab = pl.BlockSpec((tm, 2 * B_HD), lambda b, i: (i, 0))
        in_specs += [tab, tab]
        args += [cos, sin]
    out_spec = pl.BlockSpec((1, tm, B_WIDTH), lambda b, i: (b, i, 0))
    out_shape = [jax.ShapeDtypeStruct((bsz, t_len, B_WIDTH), BF16)] * 2
    out_specs = [out_spec] * 2
    if emit_cache:
        out_shape = out_shape + [jax.ShapeDtypeStruct((bsz, t_len, B_WIDTH), F32)]
        out_specs = out_specs + [out_spec]
    return pl.pallas_call(
        functools.partial(_qk_prep_kernel, rope=rope, emit_cache=emit_cache),
        grid=(bsz, t_len // tm),
        in_specs=in_specs, out_specs=out_specs, out_shape=out_shape,
        compiler_params=_params("parallel", "parallel"),
        name="qk_prep",
    )(*args)


def _attn_kernel(*refs, lam_init, has_ctx):
    lam_ref, q_ref, k_ref, v_ref = refs[:4]
    pos = 4
    if has_ctx:
        ck_ref, cv_ref = refs[pos:pos + 2]
        pos += 2
    gb_ref, sub_ref, o_ref, k_s, v_s = refs[pos:pos + 5]
    width = 2 * B_HD
    t_len = k_ref.shape[1]
    tq = q_ref.shape[1]

    @pl.when(pl.program_id(2) == 0)
    def _():
        k_s[0:t_len, :] = k_ref[0]
        v_s[0:t_len, 0:width] = v_ref[0]
        if has_ctx:
            k_s[t_len:, :] = ck_ref[0]
            v_s[t_len:, 0:width] = cv_ref[0]
        v_s[:, width:] = jnp.ones((v_s.shape[0], width), BF16)

    q = q_ref[0]
    lane = lax.broadcasted_iota(jnp.int32, q.shape, 1)
    zero = jnp.zeros_like(q)
    q2 = jnp.concatenate([jnp.where(lane < B_HD, q, zero), jnp.where(lane < B_HD, zero, q)], axis=0)
    n_groups = 2 * tq // ATTN_ROWS
    scores = lambda g: _dot(q2[g * ATTN_ROWS:(g + 1) * ATTN_ROWS], k_s[...], _NT)

    def softmax_pv(s):
        m = jnp.max(s, axis=-1, keepdims=True)
        acc = _dot(jnp.exp2(s - m).astype(BF16), v_s[...])
        return acc[:, :width] / acc[:, width:]

    parts = []
    s_next = scores(0)
    for g in range(n_groups):
        s_cur = s_next
        if g + 1 < n_groups:
            s_next = scores(g + 1)
        parts.append(softmax_pv(s_cur))
    comp = jnp.concatenate(parts, axis=0)

    lv = lam_ref[...]
    lam = (jnp.exp(jnp.sum(lv[0:1] * lv[1:2], axis=-1, keepdims=True))
           - jnp.exp(jnp.sum(lv[2:3] * lv[3:4], axis=-1, keepdims=True)) + lam_init)
    o = comp[:tq] - lam * comp[tq:]
    ms = jnp.mean(o * o, axis=-1, keepdims=True)
    o = o * lax.rsqrt(ms + NORM_EPS) * sub_ref[...] * (1.0 - lam_init)
    gb = gb_ref[0].astype(F32)
    o_ref[0] = (o * (gb * jax.nn.sigmoid(gb))).astype(BF16)


def _attention(lam_vec, q, k, ctx_k, ctx_v, proj, subln, lam_init, tq):
    bsz, t_len, _ = q.shape
    has_ctx = ctx_k is not None
    width = 2 * B_HD
    qspec = pl.BlockSpec((1, tq, width), lambda b, h, i: (b, i, h))
    kspec = pl.BlockSpec((1, t_len, width), lambda b, h, i: (b, 0, h))
    vspec = pl.BlockSpec((1, t_len, width), lambda b, h, i: (b, 0, 6 * B_HEADS + h))
    in_specs = [pl.BlockSpec((4, B_HD), lambda b, h, i: (0, 0)), qspec, kspec, vspec]
    args = [lam_vec, q, k, proj]
    if has_ctx:
        cspec = pl.BlockSpec((1, ctx_k.shape[1], width), lambda b, h, i: (b, 0, h))
        in_specs += [cspec, cspec]
        args += [ctx_k, ctx_v]
    in_specs += [pl.BlockSpec((1, tq, width), lambda b, h, i: (b, i, 7 * B_HEADS + h)),
                 pl.BlockSpec((1, width), lambda b, h, i: (0, 0))]
    args += [proj, subln]
    n_keys = t_len + (ctx_k.shape[1] if has_ctx else 0)
    return pl.pallas_call(
        functools.partial(_attn_kernel, lam_init=lam_init, has_ctx=has_ctx),
        grid=(bsz, B_HEADS, t_len // tq),
        in_specs=in_specs, out_specs=qspec,
        out_shape=jax.ShapeDtypeStruct((bsz, t_len, B_WIDTH), BF16),
        scratch_shapes=[pltpu.VMEM((n_keys, width), BF16), pltpu.VMEM((n_keys, 2 * width), BF16)],
        compiler_params=_params("parallel", "parallel", "arbitrary"),
        name="diff_attn",
    )(*args)


def _rwkv_kernel(*refs, n_steps, n_sub, reverse):
    (rkv_ref, rkv_p_ref, rkv_n_ref, lo_ref, lo_p_ref, lo_n_ref, mu_rkv_ref, mu_lo_ref,
     w0_ref, wup_ref, a0_ref, aup_ref, kk_ref, ka_ref, rk_ref, s0_ref) = refs[:16]
    if reverse:
        a0f_ref, aupf_ref, of_ref, o_ref, bon_ref, s_ref, xpad_s, st_s = refs[16:]
    else:
        o_ref, s_ref, xpad_s, st_s = refs[16:]
    C = CHUNK
    R = n_sub * C
    PAD = SUBLANES
    c = pl.program_id(1)
    blk = n_steps - 1 - c if reverse else c
    first = blk == 0
    last = blk == n_steps - 1
    row = lax.broadcasted_iota(jnp.int32, (R, 1), 0)
    up = lambda ref: ref[0].astype(F32)

    block = {}

    def block_values():
        if block:
            return block["logw"], block["a"]
        lo = up(lo_ref)
        lo_prev = jnp.where(first, 0.0, up(lo_p_ref)[HALO - 1:HALO, :])
        lo_next = jnp.where(last, 0.0, up(lo_n_ref)[0:1, :])
        lo_p = jnp.where(row == 0, lo_prev, pltpu.roll(lo, 1, 0))
        lo_n = jnp.where(row == R - 1, lo_next, pltpu.roll(lo, R - 1, 0))
        lo = lo + mu_lo_ref[...] * (0.5 * (lo_p + lo_n) - lo)
        dcol = 1 if reverse else 0
        wl = lo[:, dcol * LORA:(dcol + 1) * LORA]
        al = lo[:, (2 + dcol) * LORA:(3 + dcol) * LORA]
        xw = w0_ref[...] + _dot(jnp.tanh(wl), wup_ref[...], precision=PH)
        block["logw"] = -math.exp(-0.5) * jax.nn.sigmoid(xw)
        block["a"] = jax.nn.sigmoid(a0_ref[...] + _dot(al, aup_ref[...], precision=PH))
        if reverse:
            al_f = lo[:, 2 * LORA:3 * LORA]
            block["a_sum"] = block["a"] + jax.nn.sigmoid(a0f_ref[...] + _dot(al_f, aupf_ref[...], precision=PH))
        xpad_s[PAD:PAD + R, :] = up(rkv_ref)
        xpad_s[PAD - 1:PAD, :] = jnp.where(first, 0.0, up(rkv_p_ref)[HALO - 1:HALO, :])
        xpad_s[PAD + R:PAD + R + 1, :] = jnp.where(last, 0.0, up(rkv_n_ref)[0:1, :])
        return block["logw"], block["a"]

    ti = lax.broadcasted_iota(jnp.int32, (C, C), 0)
    si = lax.broadcasted_iota(jnp.int32, (C, C), 1)
    incl = (si >= ti) if reverse else (si <= ti)
    tri = jnp.where(incl, 1.0, 0.0).astype(BF16)

    @pl.when(c == 0)
    def _():
        st_s[...] = s0_ref[0]

    W2 = 2 * A_HEAD
    lane = lax.broadcasted_iota(jnp.int32, (C, W2), 1)
    trow = lax.broadcasted_iota(jnp.int32, (C, W2), 0)
    low = lane < A_HEAD
    before2 = (lane % A_HEAD - trow) if reverse else (trow - lane % A_HEAD)
    strict2 = before2 > 0
    incl2 = before2 >= 0
    eye2 = jnp.where(before2 == 0, 1.0, 0.0)
    brow = lax.broadcasted_iota(jnp.int32, (W2, W2), 0)
    bcol = lax.broadcasted_iota(jnp.int32, (W2, W2), 1)
    bdmask = (brow < A_HEAD) == (bcol < A_HEAD)
    bdmask2 = jnp.concatenate([bdmask, bdmask], axis=1)
    eye_bd = jnp.where(brow == bcol, 1.0, 0.0)

    def bd(x):
        return jnp.where(bdmask if x.shape[1] == W2 else bdmask2, jnp.concatenate([x, x], axis=0), 0.0)

    H = C // 2
    lane_h = lax.broadcasted_iota(jnp.int32, (H, W2), 1)
    near = (lane_h % A_HEAD) < H
    eye4 = jnp.where(lax.broadcasted_iota(jnp.int32, (H, W2), 0) == lane_h % H, 1.0, 0.0)
    mask4 = brow // H == bcol // H
    shift4 = (brow // H + 1 == bcol // H) if reverse else (brow // H == bcol // H + 1)

    def bd4(x, mask=mask4):
        return jnp.where(mask, jnp.concatenate([x] * 4, axis=0), 0.0)

    def head_sum(x):
        s0 = jnp.sum(jnp.where(low, x, 0.0), axis=-1, keepdims=True)
        s1 = jnp.sum(jnp.where(low, 0.0, x), axis=-1, keepdims=True)
        return jnp.where(low, s0, s1)

    def rows(*xs):
        return jnp.concatenate(xs, axis=0)

    n_pairs = A_HEADS // 2
    state = [st_s[p] for p in range(n_pairs)]

    def prepare(j, p):
        rs = slice(j * C, (j + 1) * C)
        sl = slice(p * W2, (p + 1) * W2)
        logw, a = block_values()

        def shifted(col0):
            cs = slice(col0 + p * W2, col0 + (p + 1) * W2)
            base = PAD + j * C
            x = xpad_s[base:base + C, cs]
            nb = xpad_s[base - 1:base - 1 + C, cs] + xpad_s[base + 1:base + 1 + C, cs]
            return x + mu_rkv_ref[:, cs] * (0.5 * nb - x)

        r = shifted(0)
        k = shifted(A_WIDTH)
        vm = shifted(2 * A_WIDTH)
        lw = logw[rs, sl]
        av = a[rs, sl]
        kd = k * (1.0 + (av - 1.0) * ka_ref[:, sl])
        kk = k * kk_ref[:, sl]
        kk = kk / jnp.maximum(jnp.sqrt(head_sum(kk * kk)), 1e-12)
        if reverse:
            k_both = k * (2.0 + (block["a_sum"][rs, sl] - 2.0) * ka_ref[:, sl])
            bon_ref[0, rs, sl] = head_sum(r * k_both * rk_ref[:, sl]) * vm
        t0 = lw.astype(BF16)
        e0 = lw - t0.astype(F32)
        t1 = e0.astype(BF16)
        t2 = (e0 - t1.astype(F32)).astype(BF16)
        cum = _dot(tri, t0) + _dot(tri, t1) + _dot(tri, t2)
        pinv = jnp.exp(-cum)
        pt = jnp.exp(jnp.sum(lw, axis=0, keepdims=True))
        return kk * jnp.exp(cum - lw), kk * av * pinv, kd * pinv, r * jnp.exp(cum), vm, pt

    def solve(j, p, am, bm, kp, rm, vm, pt):
        rs = slice(j * C, (j + 1) * C)
        sl = slice(p * W2, (p + 1) * W2)
        ar = rows(am, rm)
        xb = _dot(ar, bd(bm), _NT, PH)
        xk = _dot(ar, bd(kp), _NT, PH)
        yield
        nab = jnp.where(strict2, -xb[:C], 0.0)
        mak = jnp.where(strict2, xk[:C], 0.0)
        mrb = jnp.where(incl2, xb[C:], 0.0)
        mrk = jnp.where(incl2, xk[C:], 0.0)
        np4 = jnp.where(near, nab[:H], nab[H:])
        t4 = eye4 + np4
        pw = _dot(np4, bd4(np4), precision=PH)
        mv = _dot(rows(mak, mrk), bd(vm), precision=PH)
        yield
        for _ in range(int(math.log2(H)) - 2):
            both = _dot(rows(t4, pw), bd4(pw), precision=PH)
            t4 = t4 + both[:H]
            pw = both[H:]
            yield
        t4 = t4 + _dot(t4, bd4(pw), precision=PH)
        yield
        n_off = jnp.where(near, 0.0, nab[:H]) if reverse else jnp.where(near, nab[H:], 0.0)
        x_off = _dot(n_off, bd4(t4), precision=PH)
        yield
        t_off = _dot(t4, bd4(x_off, shift4), precision=PH)
        if reverse:
            tinv = rows(jnp.where(near, t4, t_off), jnp.where(near, 0.0, t4))
        else:
            tinv = rows(jnp.where(near, t4, 0.0), jnp.where(near, t_off, t4))
        yield
        wu = jnp.concatenate([_dot(tinv, bd(am), precision=PH), _dot(tinv, bd(mv[:C]), precision=PH)], axis=1)
        yield
        mw = _dot(mrb, bd(wu), precision=PH)
        bwu = _dot(bm * pt, wu, _TN, PH)
        kv = _dot(kp * pt, vm, _TN, PH)
        yield
        rp = rm - mw[:, :W2]
        o0 = mv[C:] - mw[:, W2:]
        gt = eye_bd * pt - jnp.where(bdmask, bwu[:, :W2], 0.0)
        ht = jnp.where(bdmask, kv - bwu[:, W2:], 0.0)
        oc = _dot(rows(rp, gt), state[p], precision=PH)
        o_ref[0, rs, sl] = (oc[:C] + o0 + of_ref[0, rs, sl]) if reverse else (oc[:C] + o0)
        state[p] = oc[C:] + ht

    order = range(n_sub - 1, -1, -1) if reverse else range(n_sub)
    items = [(j, p) for j in order for p in range(n_pairs)]
    operands = [prepare(j, p) for j, p in items]
    active = [solve(j, p, *ops) for (j, p), ops in zip(items, operands)]
    while active:
        active = [gen for gen in active if next(gen, "done") != "done"]
    for p in range(n_pairs):
        st_s[p] = state[p]

    @pl.when(c == n_steps - 1)
    def _():
        s_ref[0] = st_s[...]


_STATE_BD = (A_HEADS // 2, 2 * A_HEAD, 2 * A_HEAD)


def _state_to_bd(s):
    lead = s.shape[:-3]
    st = jnp.swapaxes(s, -1, -2).reshape(lead + (A_HEADS // 2, 2, A_HEAD, A_HEAD))
    z = jnp.zeros_like(st[..., 0, :, :])
    top = jnp.concatenate([st[..., 0, :, :], z], axis=-1)
    bot = jnp.concatenate([z, st[..., 1, :, :]], axis=-1)
    return jnp.concatenate([top, bot], axis=-2)


def _state_from_bd(st):
    pair = jnp.stack([st[..., :A_HEAD, :A_HEAD], st[..., A_HEAD:, A_HEAD:]], axis=-3)
    return jnp.swapaxes(pair.reshape(st.shape[:-3] + (A_HEADS, A_HEAD, A_HEAD)), -1, -2)


def _rwkv(proj, s0, mu_rkv, mu_lo, w0, w_up, a0, a_up, k_k, k_a, r_k, forward=None):
    reverse = forward is not None
    bsz, t_len, _ = proj.shape
    n_sub = next(n for n in (RWKV_SUB, 2, 1) if t_len % (n * CHUNK) == 0)
    rows = n_sub * CHUNK
    n_steps = t_len // rows
    n_halo = t_len // HALO
    per = rows // HALO
    blk = (lambda c: n_steps - 1 - c) if reverse else (lambda c: c)
    rkv_w = 3 * A_WIDTH
    lo_w = 4 * LORA
    lo_col = (8 * D_MODEL) // lo_w

    def main(width, col):
        return pl.BlockSpec((1, rows, width), lambda b, c: (b, blk(c), col))

    def prev(width, col):
        return pl.BlockSpec((1, HALO, width), lambda b, c: (b, jnp.maximum(blk(c) * per - 1, 0), col))

    def nxt(width, col):
        return pl.BlockSpec((1, HALO, width), lambda b, c: (b, jnp.minimum((blk(c) + 1) * per, n_halo - 1), col))

    vec = lambda width: pl.BlockSpec((1, width), lambda b, c: (0, 0))
    mat = pl.BlockSpec((LORA, A_WIDTH), lambda b, c: (0, 0))
    sspec = pl.BlockSpec((1,) + _STATE_BD, lambda b, c: (b, 0, 0, 0))
    ospec = pl.BlockSpec((1, rows, A_WIDTH), lambda b, c: (b, blk(c), 0))
    big = jax.ShapeDtypeStruct((bsz, t_len, A_WIDTH), F32)
    state = jax.ShapeDtypeStruct((bsz,) + _STATE_BD, F32)
    in_specs = [main(rkv_w, 0), prev(rkv_w, 0), nxt(rkv_w, 0),
                main(lo_w, lo_col), prev(lo_w, lo_col), nxt(lo_w, lo_col),
                vec(rkv_w), vec(lo_w), vec(A_WIDTH), mat, vec(A_WIDTH), mat,
                vec(A_WIDTH), vec(A_WIDTH), vec(A_WIDTH), sspec]
    args = [proj, proj, proj, proj, proj, proj, mu_rkv, mu_lo, w0, w_up, a0, a_up, k_k, k_a, r_k, s0]
    if reverse:
        in_specs += [vec(A_WIDTH), mat, ospec]
        args += list(forward)
    return pl.pallas_call(
        functools.partial(_rwkv_kernel, n_steps=n_steps, n_sub=n_sub, reverse=reverse),
        grid=(bsz, n_steps),
        in_specs=in_specs,
        out_specs=[ospec, ospec, sspec] if reverse else [ospec, sspec],
        out_shape=[big, big, state] if reverse else [big, state],
        scratch_shapes=[pltpu.VMEM((rows + 2 * SUBLANES, rkv_w), F32), pltpu.VMEM(_STATE_BD, F32)],
        compiler_params=_params("parallel", "arbitrary"),
        name="rwkv_bwd" if reverse else "rwkv_fwd",
    )(*args)


def _even_out_kernel(oa_ref, bon_ref, ga_ref, yb_ref, x_ref, mod_ref, lw_ref, lb_ref, w_ref, o_ref, y_s):
    o = oa_ref[0]
    bon = bon_ref[0]
    ga = ga_ref[0].astype(F32)
    gate_a = ga * jax.nn.sigmoid(ga)
    W2 = 2 * A_HEAD
    low = lax.broadcasted_iota(jnp.int32, (o.shape[0], W2), 1) < A_HEAD

    def head_mean(x):
        s0 = jnp.sum(jnp.where(low, x, 0.0), axis=-1, keepdims=True)
        s1 = jnp.sum(jnp.where(low, 0.0, x), axis=-1, keepdims=True)
        return jnp.where(low, s0, s1) * (1.0 / A_HEAD)

    sls = [slice(p * W2, (p + 1) * W2) for p in range(A_HEADS // 2)]
    cen = [o[:, sl] for sl in sls]
    cen = [x - head_mean(x) for x in cen]
    var = [head_mean(x * x) for x in cen]
    for sl, x, vr in zip(sls, cen, var):
        yh = x * lax.rsqrt(vr + GN_EPS) * lw_ref[:, sl] + lb_ref[:, sl]
        y_s[:, sl] = ((yh + bon[:, sl]) * gate_a[:, sl]).astype(BF16)
    y_s[:, A_WIDTH:] = yb_ref[0]
    out = _dot(y_s[...], w_ref[...])
    gate = mod_ref[0][:, 2 * D_MODEL:]
    o_ref[0] = x_ref[0] + gate * out


def _even_out(o, bon, proj, yb, x, mods, mod_base, mod_stride, lnx_w, lnx_b, w_out, tm):
    bsz, t_len, _ = x.shape
    row = lambda col: pl.BlockSpec((1, tm, D_MODEL), lambda b, i: (b, i, col))
    vec = pl.BlockSpec((1, A_WIDTH), lambda b, i: (0, 0))
    return pl.pallas_call(
        _even_out_kernel,
        grid=(bsz, t_len // tm),
        in_specs=[row(0), row(0), row(3), row(0), row(0),
                  pl.BlockSpec((1, 1, 3 * D_MODEL), lambda b, i: (mod_base + mod_stride * b, 0, 0)),
                  vec, vec, pl.BlockSpec((A_WIDTH + B_WIDTH, D_MODEL), lambda b, i: (0, 0))],
        out_specs=row(0),
        out_shape=jax.ShapeDtypeStruct(x.shape, F32),
        scratch_shapes=[pltpu.VMEM((tm, A_WIDTH + B_WIDTH), BF16)],
        compiler_params=_params("parallel", "parallel"),
        name="even_out",
    )(o, bon, proj, yb, x, mods, lnx_w, lnx_b, w_out)


def _odd_out_kernel(bg_ref, cg_ref, u_ref, z_ref, cgp_ref, up_ref, cgn_ref, un_ref, cw_ref, cb_ref,
                    x_ref, mod_ref, w_ref, o_ref):
    i = pl.program_id(1)
    tm = bg_ref.shape[1]
    row = lax.broadcasted_iota(jnp.int32, (tm, 1), 0)
    f32 = lambda ref: ref[0].astype(F32)
    cu = f32(cg_ref) * f32(u_ref)
    prev_row = jnp.where(i == 0, 0.0, (f32(cgp_ref) * f32(up_ref))[HALO - 1:HALO, :])
    next_row = jnp.where(i == pl.num_programs(1) - 1, 0.0, (f32(cgn_ref) * f32(un_ref))[0:1, :])
    cu_prev = jnp.where(row == 0, prev_row, pltpu.roll(cu, 1, 0))
    cu_next = jnp.where(row == tm - 1, next_row, pltpu.roll(cu, tm - 1, 0))
    cw = cw_ref[...]
    conv = cu_prev * cw[0:1] + cu * cw[1:2] + cu_next * cw[2:3] + cb_ref[...]
    z = f32(z_ref)
    y = f32(bg_ref) * conv * (z * jax.nn.sigmoid(z))
    out = _dot(y.astype(BF16), w_ref[...])
    gate = mod_ref[0][:, 2 * D_MODEL:]
    o_ref[0] = x_ref[0] + gate * out


def _odd_out(proj, x, mods, mod_base, mod_stride, conv_w, conv_b, w_out, tm):
    bsz, t_len, _ = x.shape
    per = tm // HALO
    n_halo = t_len // HALO
    main = lambda col: pl.BlockSpec((1, tm, C_WIDTH), lambda b, i: (b, i, col))
    prev = lambda col: pl.BlockSpec((1, HALO, C_WIDTH), lambda b, i: (b, jnp.maximum(i * per - 1, 0), col))
    nxt = lambda col: pl.BlockSpec((1, HALO, C_WIDTH),
                                   lambda b, i: (b, jnp.minimum((i + 1) * per, n_halo - 1), col))
    xspec = pl.BlockSpec((1, tm, D_MODEL), lambda b, i: (b, i, 0))
    return pl.pallas_call(
        _odd_out_kernel,
        grid=(bsz, t_len // tm),
        in_specs=[main(0), main(1), main(2), main(3), prev(1), prev(2), nxt(1), nxt(2),
                  pl.BlockSpec(conv_w.shape, lambda b, i: (0, 0)),
                  pl.BlockSpec((1, C_WIDTH), lambda b, i: (0, 0)),
                  xspec,
                  pl.BlockSpec((1, 1, 3 * D_MODEL), lambda b, i: (mod_base + mod_stride * b, 0, 0)),
                  pl.BlockSpec((C_WIDTH, D_MODEL), lambda b, i: (0, 0))],
        out_specs=xspec,
        out_shape=jax.ShapeDtypeStruct(x.shape, F32),
        compiler_params=_params("parallel", "parallel"),
        name="odd_out",
    )(proj, proj, proj, proj, proj, proj, proj, proj, conv_w, conv_b, x, mods, w_out)


def _rope_tables(t_len):
    pos = jnp.arange(t_len, dtype=jnp.int32)
    row = (pos // GRID_W).astype(F32)
    col = (pos % GRID_W).astype(F32)
    half = B_HD // 2
    inv = ROPE_BASE ** (-jnp.arange(0, half, 2, dtype=F32) / half)
    ar = row[:, None] * inv
    ac = col[:, None] * inv
    cos = jnp.concatenate([jnp.cos(ar), jnp.cos(ar), jnp.cos(ac), jnp.cos(ac)], axis=-1)
    sin = jnp.concatenate([-jnp.sin(ar), jnp.sin(ar), -jnp.sin(ac), jnp.sin(ac)], axis=-1)
    return jnp.tile(cos, (1, 2)), jnp.tile(sin, (1, 2))


def _proj_rows(x, norm_w, mods, mod_base, mod_stride, w, flatten):
    bsz, t_len, _ = x.shape
    if flatten:
        x = x.reshape(1, bsz * t_len, D_MODEL)
    rows = x.shape[1]
    tm = next(c for c in (2048, 1024, 512, 256, 128) if rows % c == 0)
    out = _in_proj(x, norm_w, mods, mod_base, mod_stride, w, tm)
    return out.reshape(bsz, t_len, -1)


def _even_layer(x, mods, mod_base, mod_stride, flatten, norm_w, w_in, w_out, rw, dp, lam_init,
                s0, ctx_k, ctx_v, rope, emit_cache):
    bsz, t_len, _ = x.shape
    proj = _proj_rows(x, norm_w, mods, mod_base, mod_stride, w_in, flatten)
    mu_rkv, mu_lo, w0, w_up, a0, a_up, k_k, k_a, r_k, lnx_w, lnx_b = rw
    q_norm2, k_norm2, lam_vec, subln = dp
    tm = min(t_len, 256)
    cos, sin = rope if rope is not None else (None, None)
    prep = _qk_prep(proj, q_norm2, k_norm2, cos, sin, tm, emit_cache)
    qn, kn = prep[:2]
    yb = _attention(lam_vec, qn, kn, ctx_k, ctx_v, proj, subln, lam_init, min(t_len, 1024))
    scan = lambda d, fwd: _rwkv(proj, s0[d], mu_rkv, mu_lo, w0[d:d + 1], w_up[d], a0[d:d + 1], a_up[d],
                                k_k, k_a, r_k, forward=fwd)
    o_f, s_f = scan(0, None)
    o, bon, s_b = scan(1, (a0[0:1], a_up[0], o_f))
    s_new = jnp.stack([s_f, s_b], axis=0)
    x_new = _even_out(o, bon, proj, yb, x, mods, mod_base, mod_stride, lnx_w, lnx_b, w_out, min(t_len, 512))
    k_cache = prep[2] if emit_cache else None
    v_cache = proj[:, :, 6 * D_MODEL:7 * D_MODEL].astype(F32) if emit_cache else None
    return x_new, s_new, k_cache, v_cache


def _odd_layer(x, mods, mod_base, mod_stride, flatten, norm_w, w_in, conv_w, conv_b, w_out):
    proj = _proj_rows(x, norm_w, mods, mod_base, mod_stride, w_in, flatten)
    tm = min(x.shape[1], 512)
    return _odd_out(proj, x, mods, mod_base, mod_stride, conv_w, conv_b, w_out, tm)


def kernel(x_prompt, x_sample, state_rwkv_fwd, state_rwkv_bwd, cache_diff_k, cache_diff_v, c, c_ctx,
           norm_w, ada_w, ada_b,
           e_w_in, e_w_out, e_mu, e_w0, e_w_up, e_a0, e_a_up, e_k_k, e_k_a, e_r_k, e_lnx_w, e_lnx_b,
           e_q_norm, e_k_norm, e_lambda, e_subln,
           o_w_in, o_conv_w, o_conv_b, o_w_out):
    depth = norm_w.shape[0]
    bsz, seq = x_prompt.shape[:2]
    dec_b, dec_t = x_sample.shape[:2]
    past = cache_diff_k.shape[2]
    assert dec_b + 1 <= MOD_ROWS
    cvec = jnp.concatenate([c_ctx[None, :], c, jnp.zeros((MOD_ROWS - 1 - dec_b, D_MODEL), F32)], axis=0)
    mods = _ada(cvec, ada_w, ada_b)
    rope = _rope_tables(dec_t)
    rkv_w = 3 * A_WIDTH
    lora_end = rkv_w + 4 * LORA

    xp, xs = x_prompt, x_sample
    new_sf, new_sb, new_k, new_v = [], [], [], []
    for layer in range(depth):
        i = layer // 2
        nw = norm_w[layer][None, :]
        base = layer * MOD_ROWS
        if layer % 2 == 0:
            lam_init = 0.8 - 0.6 * math.exp(-0.3 * layer)
            w_in = jnp.concatenate([e_w_in[i][:, :rkv_w], e_w_in[i][:, lora_end:], e_w_in[i][:, rkv_w:lora_end]],
                                   axis=1).astype(BF16)
            w_out = e_w_out[i].astype(BF16)
            row = lambda t: t.reshape(1, -1)
            rw = (row(e_mu[i][:rkv_w]), row(e_mu[i][rkv_w:lora_end]), e_w0[i], e_w_up[i], e_a0[i], e_a_up[i],
                  row(e_k_k[i]), row(e_k_a[i]), row(e_r_k[i]), row(e_lnx_w[i]), row(e_lnx_b[i]))
            dp = (jnp.tile(e_q_norm[i], 2)[None, :], jnp.tile(e_k_norm[i], 2)[None, :], e_lambda[i],
                  row(e_subln[i]))
            zeros = jnp.zeros((2, bsz) + _STATE_BD, F32)
            xp, s_p, k_c, v_c = _even_layer(xp, mods, base, 0, True, nw, w_in, w_out, rw, dp, lam_init,
                                            zeros, None, None, None, True)
            s_p = _state_from_bd(s_p)
            s0 = _state_to_bd(jnp.stack([state_rwkv_fwd[:, i], state_rwkv_bwd[:, i]], axis=0))
            ctx_k = cache_diff_k[:, i].reshape(dec_b, past, B_WIDTH).astype(BF16)
            ctx_v = cache_diff_v[:, i].reshape(dec_b, past, B_WIDTH).astype(BF16)
            xs, _, _, _ = _even_layer(xs, mods, base + 1, 1, False, nw, w_in, w_out, rw, dp, lam_init,
                                      s0, ctx_k, ctx_v, rope, False)
            new_sf.append(s_p[0])
            new_sb.append(s_p[1])
            new_k.append(k_c.reshape(bsz, seq, B_HEADS, 2, B_HD))
            new_v.append(v_c.reshape(bsz, seq, B_HEADS, 2 * B_HD))
        else:
            w_in = o_w_in[i].astype(BF16)
            w_out = o_w_out[i].astype(BF16)
            cb = o_conv_b[i][None, :]
            xp = _odd_layer(xp, mods, base, 0, True, nw, w_in, o_conv_w[i], cb, w_out)
            xs = _odd_layer(xs, mods, base + 1, 1, False, nw, w_in, o_conv_w[i], cb, w_out)
    return (xp, xs, jnp.stack(new_sf, axis=1), jnp.stack(new_sb, axis=1),
            jnp.stack(new_k, axis=1), jnp.stack(new_v, axis=1))
```

```python
import functools
import math

import jax
import jax.numpy as jnp
from jax import lax
from jax.experimental import pallas as pl
from jax.experimental.pallas import tpu as pltpu

F32 = jnp.float32
BF16 = jnp.bfloat16
HI = lax.Precision.HIGHEST
PH = None

D_MODEL = 1024
A_WIDTH = 1024
A_HEAD = 64
A_HEADS = A_WIDTH // A_HEAD
LORA = 64
B_HD = 64
B_HEADS = 8
B_WIDTH = 1024
C_WIDTH = 2048
GRID_W = 64
ROPE_BASE = 10000.0
NORM_EPS = 1e-6
GN_EPS = 64e-5
MOD_ROWS = 8
CHUNK = 64
RWKV_SUB = 2
SUBLANES = 8
HALO = 16
VMEM_LIMIT = 48 * 1024 * 1024
ATTN_ROWS = 128
ATTN_SMALL = 256 * 256

_NT = (((1,), (1,)), ((), ()))
_TN = (((0,), (0,)), ((), ()))


def _dot(x, y, dims=None, precision=None):
    if dims is None:
        return jnp.dot(x, y, precision=precision, preferred_element_type=F32)
    return lax.dot_general(x, y, dims, precision=precision, preferred_element_type=F32)


def _params(*sem):
    return pltpu.CompilerParams(dimension_semantics=sem, vmem_limit_bytes=VMEM_LIMIT)


def _ada_kernel(c_ref, w_ref, b_ref, o_ref):
    cv = c_ref[...]
    s = cv * jax.nn.sigmoid(cv)
    o_ref[0] = _dot(s, w_ref[0], precision=HI) + b_ref[0]


def _ada(cvec, ada_w, ada_b):
    depth = ada_w.shape[0]
    tn = D_MODEL
    out = pl.pallas_call(
        _ada_kernel,
        grid=(depth, 3 * D_MODEL // tn),
        in_specs=[pl.BlockSpec((MOD_ROWS, D_MODEL), lambda l, j: (0, 0)),
                  pl.BlockSpec((1, D_MODEL, tn), lambda l, j: (l, 0, j)),
                  pl.BlockSpec((1, 1, tn), lambda l, j: (l, 0, j))],
        out_specs=pl.BlockSpec((1, MOD_ROWS, tn), lambda l, j: (l, 0, j)),
        out_shape=jax.ShapeDtypeStruct((depth, MOD_ROWS, 3 * D_MODEL), F32),
        compiler_params=_params("parallel", "parallel"),
        name="ada",
    )(cvec, ada_w, ada_b.reshape(depth, 1, 3 * D_MODEL))
    return out.reshape(depth * MOD_ROWS, 1, 3 * D_MODEL)


def _in_proj_kernel(x_ref, nw_ref, mod_ref, w_ref, o_ref, h_ref):
    @pl.when(pl.program_id(2) == 0)
    def _():
        x = x_ref[0]
        ms = jnp.mean(x * x, axis=-1, keepdims=True)
        y = x * lax.rsqrt(ms + NORM_EPS) * nw_ref[...]
        m = mod_ref[0]
        h = y * (1.0 + m[:, D_MODEL:2 * D_MODEL]) + m[:, :D_MODEL]
        h_ref[...] = h.astype(BF16)

    o_ref[0] = _dot(h_ref[...], w_ref[...]).astype(o_ref.dtype)


def _in_proj(x, norm_w, mods, mod_base, mod_stride, w, tm):
    bsz, t_len, _ = x.shape
    n_out = w.shape[1]
    tn = next(c for c in (1024, 768, 512, 256, 128) if n_out % c == 0)
    return pl.pallas_call(
        _in_proj_kernel,
        grid=(bsz, t_len // tm, n_out // tn),
        in_specs=[pl.BlockSpec((1, tm, D_MODEL), lambda b, i, j: (b, i, 0)),
                  pl.BlockSpec((1, D_MODEL), lambda b, i, j: (0, 0)),
                  pl.BlockSpec((1, 1, 3 * D_MODEL), lambda b, i, j: (mod_base + mod_stride * b, 0, 0)),
                  pl.BlockSpec((D_MODEL, tn), lambda b, i, j: (0, j))],
        out_specs=pl.BlockSpec((1, tm, tn), lambda b, i, j: (b, i, j)),
        out_shape=jax.ShapeDtypeStruct((bsz, t_len, n_out), BF16),
        scratch_shapes=[pltpu.VMEM((tm, D_MODEL), BF16)],
        compiler_params=_params("parallel", "parallel", "arbitrary"),
        name="in_proj",
    )(x, norm_w, mods, w)


def _qk_prep_kernel(*refs, rope, emit_cache):
    pq_ref, pk_ref, qn_ref, kn_ref = refs[:4]
    pos = 4
    if rope:
        cos_ref, sin_ref = refs[pos:pos + 2]
        pos += 2
    q_out, k_out = refs[pos:pos + 2]
    kc_out = refs[pos + 2] if emit_cache else None
    width = 2 * B_HD
    tm = pq_ref.shape[1]
    lane = lax.broadcasted_iota(jnp.int32, (tm, width), 1)
    first16 = (lane % 32) < 16

    grp = lax.broadcasted_iota(jnp.int32, (width, width), 0) // B_HD
    same = jnp.where(grp == lax.broadcasted_iota(jnp.int32, (width, width), 1) // B_HD, 1.0, 0.0).astype(BF16)

    def norm(x, w):
        ss = x * x
        hi = ss.astype(BF16)
        lo = (ss - hi.astype(F32)).astype(BF16)
        ms = (_dot(hi, same) + _dot(lo, same)) * (1.0 / B_HD)
        return x * lax.rsqrt(ms + NORM_EPS) * w

    def rot(x):
        if not rope:
            return x
        partner = jnp.where(first16, pltpu.roll(x, width - 16, 1), pltpu.roll(x, 16, 1))
        return x * cos_ref[...] + partner * sin_ref[...]

    for h in range(B_HEADS):
        sl = slice(h * width, (h + 1) * width)
        q = norm(pq_ref[0, :, sl].astype(F32), qn_ref[...])
        k = norm(pk_ref[0, :, sl].astype(F32), kn_ref[...])
        if emit_cache:
            kc_out[0, :, sl] = k
        q_out[0, :, sl] = (rot(q) * (B_HD ** -0.5 * math.log2(math.e))).astype(BF16)
        k_out[0, :, sl] = rot(k).astype(BF16)


def _qk_prep(proj, q_norm2, k_norm2, cos, sin, tm, emit_cache):
    bsz, t_len, _ = proj.shape
    rope = cos is not None
    col = lambda c: pl.BlockSpec((1, tm, B_WIDTH), lambda b, i: (b, i, c))
    vec = pl.BlockSpec((1, 2 * B_HD), lambda b, i: (0, 0))
    in_specs = [col(4), col(5), vec, vec]
    args = [proj, proj, q_norm2, k_norm2]
    if rope:
        tab = pl.BlockSpec((tm, 2 * B_HD), lambda b, i: (i, 0))
        in_specs += [tab, tab]
        args += [cos, sin]
    out_spec = pl.BlockSpec((1, tm, B_WIDTH), lambda b, i: (b, i, 0))
    out_shape = [jax.ShapeDtypeStruct((bsz, t_len, B_WIDTH), BF16)] * 2
    out_specs = [out_spec] * 2
    if emit_cache:
        out_shape = out_shape + [jax.ShapeDtypeStruct((bsz, t_len, B_WIDTH), F32)]
        out_specs = out_specs + [out_spec]
    return pl.pallas_call(
        functools.partial(_qk_prep_kernel, rope=rope, emit_cache=emit_cache),
        grid=(bsz, t_len // tm),
        in_specs=in_specs, out_specs=out_specs, out_shape=out_shape,
        compiler_params=_params("parallel", "parallel"),
        name="qk_prep",
    )(*args)


def _attn_kernel(*refs, lam_init, has_ctx):
    lam_ref, q_ref, k_ref, v_ref = refs[:4]
    pos = 4
    if has_ctx:
        ck_ref, cv_ref = refs[pos:pos + 2]
        pos += 2
    gb_ref, sub_ref, o_ref, k_s, v_s = refs[pos:pos + 5]
    width = 2 * B_HD
    t_len = k_ref.shape[1]
    tq = q_ref.shape[1]

    heads = q_ref.shape[2] // width

    @pl.when(pl.program_id(2) == 0)
    def _():
        for h in range(heads):
            sl = slice(h * width, (h + 1) * width)
            k_s[h, 0:t_len, :] = k_ref[0, :, sl]
            v_s[h, 0:t_len, 0:width] = v_ref[0, :, sl]
            if has_ctx:
                k_s[h, t_len:, :] = ck_ref[0, :, sl]
                v_s[h, t_len:, 0:width] = cv_ref[0, :, sl]
        v_s[:, :, width:] = jnp.ones(v_s.shape[:2] + (width,), BF16)

    lv = lam_ref[...]
    lam = (jnp.exp(jnp.sum(lv[0:1] * lv[1:2], axis=-1, keepdims=True))
           - jnp.exp(jnp.sum(lv[2:3] * lv[3:4], axis=-1, keepdims=True)) + lam_init)
    lane = lax.broadcasted_iota(jnp.int32, (tq, width), 1)
    n_groups = 2 * tq // ATTN_ROWS

    for h in range(heads):
        sl = slice(h * width, (h + 1) * width)
        q = q_ref[0, :, sl]
        zero = jnp.zeros_like(q)
        q2 = jnp.concatenate([jnp.where(lane < B_HD, q, zero), jnp.where(lane < B_HD, zero, q)], axis=0)
        scores = lambda g: _dot(q2[g * ATTN_ROWS:(g + 1) * ATTN_ROWS], k_s[h], _NT)

        def softmax_pv(s):
            m = jnp.max(s, axis=-1, keepdims=True)
            acc = _dot(jnp.exp2(s - m).astype(BF16), v_s[h])
            return acc[:, :width] / acc[:, width:]

        parts = []
        s_next = scores(0)
        for g in range(n_groups):
            s_cur = s_next
            if g + 1 < n_groups:
                s_next = scores(g + 1)
            parts.append(softmax_pv(s_cur))
        comp = jnp.concatenate(parts, axis=0)
        o = comp[:tq] - lam * comp[tq:]
        ms = jnp.mean(o * o, axis=-1, keepdims=True)
        o = o * lax.rsqrt(ms + NORM_EPS) * sub_ref[...] * (1.0 - lam_init)
        gb = gb_ref[0, :, sl].astype(F32)
        o_ref[0, :, sl] = (o * (gb * jax.nn.sigmoid(gb))).astype(BF16)


def _attention(lam_vec, q, k, ctx_k, ctx_v, proj, subln, lam_init, tq):
    bsz, t_len, _ = q.shape
    has_ctx = ctx_k is not None
    width = 2 * B_HD
    n_keys = t_len + (ctx_k.shape[1] if has_ctx else 0)
    hp = B_HEADS if n_keys * tq <= ATTN_SMALL else 1
    span = hp * width
    per = B_HEADS // hp
    qspec = pl.BlockSpec((1, tq, span), lambda b, h, i: (b, i, h))
    kspec = pl.BlockSpec((1, t_len, span), lambda b, h, i: (b, 0, h))
    vspec = pl.BlockSpec((1, t_len, span), lambda b, h, i: (b, 0, 6 * per + h))
    in_specs = [pl.BlockSpec((4, B_HD), lambda b, h, i: (0, 0)), qspec, kspec, vspec]
    args = [lam_vec, q, k, proj]
    if has_ctx:
        cspec = pl.BlockSpec((1, ctx_k.shape[1], span), lambda b, h, i: (b, 0, h))
        in_specs += [cspec, cspec]
        args += [ctx_k, ctx_v]
    in_specs += [pl.BlockSpec((1, tq, span), lambda b, h, i: (b, i, 7 * per + h)),
                 pl.BlockSpec((1, width), lambda b, h, i: (0, 0))]
    args += [proj, subln]
    return pl.pallas_call(
        functools.partial(_attn_kernel, lam_init=lam_init, has_ctx=has_ctx),
        grid=(bsz, per, t_len // tq),
        in_specs=in_specs, out_specs=qspec,
        out_shape=jax.ShapeDtypeStruct((bsz, t_len, B_WIDTH), BF16),
        scratch_shapes=[pltpu.VMEM((hp, n_keys, width), BF16), pltpu.VMEM((hp, n_keys, 2 * width), BF16)],
        compiler_params=_params("parallel", "parallel", "arbitrary"),
        name="diff_attn",
    )(*args)


def _rwkv_kernel(*refs, n_steps, n_sub, reverse):
    (rkv_ref, rkv_p_ref, rkv_n_ref, lo_ref, lo_p_ref, lo_n_ref, mu_rkv_ref, mu_lo_ref,
     w0_ref, wup_ref, a0_ref, aup_ref, kk_ref, ka_ref, rk_ref, s0_ref) = refs[:16]
    if reverse:
        a0f_ref, aupf_ref, of_ref, o_ref, bon_ref, s_ref, xpad_s, st_s = refs[16:]
    else:
        o_ref, s_ref, xpad_s, st_s = refs[16:]
    C = CHUNK
    R = n_sub * C
    PAD = SUBLANES
    c = pl.program_id(1)
    blk = n_steps - 1 - c if reverse else c
    first = blk == 0
    last = blk == n_steps - 1
    row = lax.broadcasted_iota(jnp.int32, (R, 1), 0)
    up = lambda ref: ref[0].astype(F32)

    block = {}

    def block_values():
        if block:
            return block["logw"], block["a"]
        lo = up(lo_ref)
        lo_prev = jnp.where(first, 0.0, up(lo_p_ref)[HALO - 1:HALO, :])
        lo_next = jnp.where(last, 0.0, up(lo_n_ref)[0:1, :])
        lo_p = jnp.where(row == 0, lo_prev, pltpu.roll(lo, 1, 0))
        lo_n = jnp.where(row == R - 1, lo_next, pltpu.roll(lo, R - 1, 0))
        lo = lo + mu_lo_ref[...] * (0.5 * (lo_p + lo_n) - lo)
        dcol = 1 if reverse else 0
        wl = lo[:, dcol * LORA:(dcol + 1) * LORA]
        al = lo[:, (2 + dcol) * LORA:(3 + dcol) * LORA]
        xw = w0_ref[...] + _dot(jnp.tanh(wl), wup_ref[...], precision=PH)
        block["logw"] = -math.exp(-0.5) * jax.nn.sigmoid(xw)
        block["a"] = jax.nn.sigmoid(a0_ref[...] + _dot(al, aup_ref[...], precision=PH))
        if reverse:
            al_f = lo[:, 2 * LORA:3 * LORA]
            block["a_sum"] = block["a"] + jax.nn.sigmoid(a0f_ref[...] + _dot(al_f, aupf_ref[...], precision=PH))
        xpad_s[PAD:PAD + R, :] = up(rkv_ref)
        xpad_s[PAD - 1:PAD, :] = jnp.where(first, 0.0, up(rkv_p_ref)[HALO - 1:HALO, :])
        xpad_s[PAD + R:PAD + R + 1, :] = jnp.where(last, 0.0, up(rkv_n_ref)[0:1, :])
        return block["logw"], block["a"]

    ti = lax.broadcasted_iota(jnp.int32, (C, C), 0)
    si = lax.broadcasted_iota(jnp.int32, (C, C), 1)
    incl = (si >= ti) if reverse else (si <= ti)
    tri = jnp.where(incl, 1.0, 0.0).astype(BF16)

    @pl.when(c == 0)
    def _():
        st_s[...] = s0_ref[0]

    W2 = 2 * A_HEAD
    lane = lax.broadcasted_iota(jnp.int32, (C, W2), 1)
    trow = lax.broadcasted_iota(jnp.int32, (C, W2), 0)
    low = lane < A_HEAD
    before2 = (lane % A_HEAD - trow) if reverse else (trow - lane % A_HEAD)
    strict2 = before2 > 0
    incl2 = before2 >= 0
    eye2 = jnp.where(before2 == 0, 1.0, 0.0)
    brow = lax.broadcasted_iota(jnp.int32, (W2, W2), 0)
    bcol = lax.broadcasted_iota(jnp.int32, (W2, W2), 1)
    bdmask = (brow < A_HEAD) == (bcol < A_HEAD)
    bdmask2 = jnp.concatenate([bdmask, bdmask], axis=1)
    eye_bd = jnp.where(brow == bcol, 1.0, 0.0)

    def bd(x):
        return jnp.where(bdmask if x.shape[1] == W2 else bdmask2, jnp.concatenate([x, x], axis=0), 0.0)

    H = C // 2
    lane_h = lax.broadcasted_iota(jnp.int32, (H, W2), 1)
    near = (lane_h % A_HEAD) < H
    eye4 = jnp.where(lax.broadcasted_iota(jnp.int32, (H, W2), 0) == lane_h % H, 1.0, 0.0)
    mask4 = brow // H == bcol // H
    shift4 = (brow // H + 1 == bcol // H) if reverse else (brow // H == bcol // H + 1)

    def bd4(x, mask=mask4):
        return jnp.where(mask, jnp.concatenate([x] * 4, axis=0), 0.0)

    def head_sum(x):
        s0 = jnp.sum(jnp.where(low, x, 0.0), axis=-1, keepdims=True)
        s1 = jnp.sum(jnp.where(low, 0.0, x), axis=-1, keepdims=True)
        return jnp.where(low, s0, s1)

    def rows(*xs):
        return jnp.concatenate(xs, axis=0)

    n_pairs = A_HEADS // 2
    state = [st_s[p] for p in range(n_pairs)]

    def prepare(j, p):
        rs = slice(j * C, (j + 1) * C)
        sl = slice(p * W2, (p + 1) * W2)
        logw, a = block_values()

        def shifted(col0):
            cs = slice(col0 + p * W2, col0 + (p + 1) * W2)
            base = PAD + j * C
            x = xpad_s[base:base + C, cs]
            nb = xpad_s[base - 1:base - 1 + C, cs] + xpad_s[base + 1:base + 1 + C, cs]
            return x + mu_rkv_ref[:, cs] * (0.5 * nb - x)

        r = shifted(0)
        k = shifted(A_WIDTH)
        vm = shifted(2 * A_WIDTH)
        lw = logw[rs, sl]
        av = a[rs, sl]
        kd = k * (1.0 + (av - 1.0) * ka_ref[:, sl])
        kk = k * kk_ref[:, sl]
        kk = kk / jnp.maximum(jnp.sqrt(head_sum(kk * kk)), 1e-12)
        if reverse:
            k_both = k * (2.0 + (block["a_sum"][rs, sl] - 2.0) * ka_ref[:, sl])
            bon_ref[0, rs, sl] = head_sum(r * k_both * rk_ref[:, sl]) * vm
        t0 = lw.astype(BF16)
        e0 = lw - t0.astype(F32)
        t1 = e0.astype(BF16)
        t2 = (e0 - t1.astype(F32)).astype(BF16)
        cum = _dot(tri, t0) + _dot(tri, t1) + _dot(tri, t2)
        pinv = jnp.exp(-cum)
        pt = jnp.exp(jnp.sum(lw, axis=0, keepdims=True))
        return kk * jnp.exp(cum - lw), kk * av * pinv, kd * pinv, r * jnp.exp(cum), vm, pt

    def solve(j, p, am, bm, kp, rm, vm, pt):
        rs = slice(j * C, (j + 1) * C)
        sl = slice(p * W2, (p + 1) * W2)
        ar = rows(am, rm)
        xb = _dot(ar, bd(bm), _NT, PH)
        xk = _dot(ar, bd(kp), _NT, PH)
        yield
        nab = jnp.where(strict2, -xb[:C], 0.0)
        mak = jnp.where(strict2, xk[:C], 0.0)
        mrb = jnp.where(incl2, xb[C:], 0.0)
        mrk = jnp.where(incl2, xk[C:], 0.0)
        np4 = jnp.where(near, nab[:H], nab[H:])
        t4 = eye4 + np4
        pw = _dot(np4, bd4(np4), precision=PH)
        mv = _dot(rows(mak, mrk), bd(vm), precision=PH)
        yield
        for _ in range(int(math.log2(H)) - 2):
            both = _dot(rows(t4, pw), bd4(pw), precision=PH)
            t4 = t4 + both[:H]
            pw = both[H:]
            yield
        t4 = t4 + _dot(t4, bd4(pw), precision=PH)
        yield
        n_off = jnp.where(near, 0.0, nab[:H]) if reverse else jnp.where(near, nab[H:], 0.0)
        x_off = _dot(n_off, bd4(t4), precision=PH)
        yield
        t_off = _dot(t4, bd4(x_off, shift4), precision=PH)
        if reverse:
            tinv = rows(jnp.where(near, t4, t_off), jnp.where(near, 0.0, t4))
        else:
            tinv = rows(jnp.where(near, t4, 0.0), jnp.where(near, t_off, t4))
        yield
        wu = jnp.concatenate([_dot(tinv, bd(am), precision=PH), _dot(tinv, bd(mv[:C]), precision=PH)], axis=1)
        yield
        mw = _dot(mrb, bd(wu), precision=PH)
        bwu = _dot(bm * pt, wu, _TN, PH)
        kv = _dot(kp * pt, vm, _TN, PH)
        yield
        rp = rm - mw[:, :W2]
        o0 = mv[C:] - mw[:, W2:]
        gt = eye_bd * pt - jnp.where(bdmask, bwu[:, :W2], 0.0)
        ht = jnp.where(bdmask, kv - bwu[:, W2:], 0.0)
        oc = _dot(rows(rp, gt), state[p], precision=PH)
        o_ref[0, rs, sl] = (oc[:C] + o0 + of_ref[0, rs, sl]) if reverse else (oc[:C] + o0)
        state[p] = oc[C:] + ht

    order = range(n_sub - 1, -1, -1) if reverse else range(n_sub)
    items = [(j, p) for j in order for p in range(n_pairs)]
    operands = [prepare(j, p) for j, p in items]
    active = [solve(j, p, *ops) for (j, p), ops in zip(items, operands)]
    while active:
        active = [gen for gen in active if next(gen, "done") != "done"]
    for p in range(n_pairs):
        st_s[p] = state[p]

    @pl.when(c == n_steps - 1)
    def _():
        s_ref[0] = st_s[...]


_STATE_BD = (A_HEADS // 2, 2 * A_HEAD, 2 * A_HEAD)


def _state_to_bd(s):
    lead = s.shape[:-3]
    st = jnp.swapaxes(s, -1, -2).reshape(lead + (A_HEADS // 2, 2, A_HEAD, A_HEAD))
    z = jnp.zeros_like(st[..., 0, :, :])
    top = jnp.concatenate([st[..., 0, :, :], z], axis=-1)
    bot = jnp.concatenate([z, st[..., 1, :, :]], axis=-1)
    return jnp.concatenate([top, bot], axis=-2)


def _state_from_bd(st):
    pair = jnp.stack([st[..., :A_HEAD, :A_HEAD], st[..., A_HEAD:, A_HEAD:]], axis=-3)
    return jnp.swapaxes(pair.reshape(st.shape[:-3] + (A_HEADS, A_HEAD, A_HEAD)), -1, -2)


def _rwkv(proj, s0, mu_rkv, mu_lo, w0, w_up, a0, a_up, k_k, k_a, r_k, forward=None):
    reverse = forward is not None
    bsz, t_len, _ = proj.shape
    n_sub = next(n for n in (RWKV_SUB, 2, 1) if t_len % (n * CHUNK) == 0)
    rows = n_sub * CHUNK
    n_steps = t_len // rows
    n_halo = t_len // HALO
    per = rows // HALO
    blk = (lambda c: n_steps - 1 - c) if reverse else (lambda c: c)
    rkv_w = 3 * A_WIDTH
    lo_w = 4 * LORA
    lo_col = (8 * D_MODEL) // lo_w

    def main(width, col):
        return pl.BlockSpec((1, rows, width), lambda b, c: (b, blk(c), col))

    def prev(width, col):
        return pl.BlockSpec((1, HALO, width), lambda b, c: (b, jnp.maximum(blk(c) * per - 1, 0), col))

    def nxt(width, col):
        return pl.BlockSpec((1, HALO, width), lambda b, c: (b, jnp.minimum((blk(c) + 1) * per, n_halo - 1), col))

    vec = lambda width: pl.BlockSpec((1, width), lambda b, c: (0, 0))
    mat = pl.BlockSpec((LORA, A_WIDTH), lambda b, c: (0, 0))
    sspec = pl.BlockSpec((1,) + _STATE_BD, lambda b, c: (b, 0, 0, 0))
    ospec = pl.BlockSpec((1, rows, A_WIDTH), lambda b, c: (b, blk(c), 0))
    big = jax.ShapeDtypeStruct((bsz, t_len, A_WIDTH), F32)
    state = jax.ShapeDtypeStruct((bsz,) + _STATE_BD, F32)
    in_specs = [main(rkv_w, 0), prev(rkv_w, 0), nxt(rkv_w, 0),
                main(lo_w, lo_col), prev(lo_w, lo_col), nxt(lo_w, lo_col),
                vec(rkv_w), vec(lo_w), vec(A_WIDTH), mat, vec(A_WIDTH), mat,
                vec(A_WIDTH), vec(A_WIDTH), vec(A_WIDTH), sspec]
    args = [proj, proj, proj, proj, proj, proj, mu_rkv, mu_lo, w0, w_up, a0, a_up, k_k, k_a, r_k, s0]
    if reverse:
        in_specs += [vec(A_WIDTH), mat, ospec]
        args += list(forward)
    return pl.pallas_call(
        functools.partial(_rwkv_kernel, n_steps=n_steps, n_sub=n_sub, reverse=reverse),
        grid=(bsz, n_steps),
        in_specs=in_specs,
        out_specs=[ospec, ospec, sspec] if reverse else [ospec, sspec],
        out_shape=[big, big, state] if reverse else [big, state],
        scratch_shapes=[pltpu.VMEM((rows + 2 * SUBLANES, rkv_w), F32), pltpu.VMEM(_STATE_BD, F32)],
        compiler_params=_params("parallel", "arbitrary"),
        name="rwkv_bwd" if reverse else "rwkv_fwd",
    )(*args)


def _even_out_kernel(oa_ref, bon_ref, ga_ref, yb_ref, x_ref, mod_ref, lw_ref, lb_ref, w_ref, o_ref, y_s):
    o = oa_ref[0]
    bon = bon_ref[0]
    ga = ga_ref[0].astype(F32)
    gate_a = ga * jax.nn.sigmoid(ga)
    W2 = 2 * A_HEAD
    low = lax.broadcasted_iota(jnp.int32, (o.shape[0], W2), 1) < A_HEAD

    def head_mean(x):
        s0 = jnp.sum(jnp.where(low, x, 0.0), axis=-1, keepdims=True)
        s1 = jnp.sum(jnp.where(low, 0.0, x), axis=-1, keepdims=True)
        return jnp.where(low, s0, s1) * (1.0 / A_HEAD)

    sls = [slice(p * W2, (p + 1) * W2) for p in range(A_HEADS // 2)]
    cen = [o[:, sl] for sl in sls]
    cen = [x - head_mean(x) for x in cen]
    var = [head_mean(x * x) for x in cen]
    for sl, x, vr in zip(sls, cen, var):
        yh = x * lax.rsqrt(vr + GN_EPS) * lw_ref[:, sl] + lb_ref[:, sl]
        y_s[:, sl] = ((yh + bon[:, sl]) * gate_a[:, sl]).astype(BF16)
    y_s[:, A_WIDTH:] = yb_ref[0]
    out = _dot(y_s[...], w_ref[...])
    gate = mod_ref[0][:, 2 * D_MODEL:]
    o_ref[0] = x_ref[0] + gate * out


def _even_out(o, bon, proj, yb, x, mods, mod_base, mod_stride, lnx_w, lnx_b, w_out, tm):
    bsz, t_len, _ = x.shape
    row = lambda col: pl.BlockSpec((1, tm, D_MODEL), lambda b, i: (b, i, col))
    vec = pl.BlockSpec((1, A_WIDTH), lambda b, i: (0, 0))
    return pl.pallas_call(
        _even_out_kernel,
        grid=(bsz, t_len // tm),
        in_specs=[row(0), row(0), row(3), row(0), row(0),
                  pl.BlockSpec((1, 1, 3 * D_MODEL), lambda b, i: (mod_base + mod_stride * b, 0, 0)),
                  vec, vec, pl.BlockSpec((A_WIDTH + B_WIDTH, D_MODEL), lambda b, i: (0, 0))],
        out_specs=row(0),
        out_shape=jax.ShapeDtypeStruct(x.shape, F32),
        scratch_shapes=[pltpu.VMEM((tm, A_WIDTH + B_WIDTH), BF16)],
        compiler_params=_params("parallel", "parallel"),
        name="even_out",
    )(o, bon, proj, yb, x, mods, lnx_w, lnx_b, w_out)


def _odd_out_kernel(bg_ref, cg_ref, u_ref, z_ref, cgp_ref, up_ref, cgn_ref, un_ref, cw_ref, cb_ref,
                    x_ref, mod_ref, w_ref, o_ref):
    i = pl.program_id(1)
    tm = bg_ref.shape[1]
    row = lax.broadcasted_iota(jnp.int32, (tm, 1), 0)
    f32 = lambda ref: ref[0].astype(F32)
    cu = f32(cg_ref) * f32(u_ref)
    prev_row = jnp.where(i == 0, 0.0, (f32(cgp_ref) * f32(up_ref))[HALO - 1:HALO, :])
    next_row = jnp.where(i == pl.num_programs(1) - 1, 0.0, (f32(cgn_ref) * f32(un_ref))[0:1, :])
    cu_prev = jnp.where(row == 0, prev_row, pltpu.roll(cu, 1, 0))
    cu_next = jnp.where(row == tm - 1, next_row, pltpu.roll(cu, tm - 1, 0))
    cw = cw_ref[...]
    conv = cu_prev * cw[0:1] + cu * cw[1:2] + cu_next * cw[2:3] + cb_ref[...]
    z = f32(z_ref)
    y = f32(bg_ref) * conv * (z * jax.nn.sigmoid(z))
    out = _dot(y.astype(BF16), w_ref[...])
    gate = mod_ref[0][:, 2 * D_MODEL:]
    o_ref[0] = x_ref[0] + gate * out


def _odd_out(proj, x, mods, mod_base, mod_stride, conv_w, conv_b, w_out, tm):
    bsz, t_len, _ = x.shape
    per = tm // HALO
    n_halo = t_len // HALO
    main = lambda col: pl.BlockSpec((1, tm, C_WIDTH), lambda b, i: (b, i, col))
    prev = lambda col: pl.BlockSpec((1, HALO, C_WIDTH), lambda b, i: (b, jnp.maximum(i * per - 1, 0), col))
    nxt = lambda col: pl.BlockSpec((1, HALO, C_WIDTH),
                                   lambda b, i: (b, jnp.minimum((i + 1) * per, n_halo - 1), col))
    xspec = pl.BlockSpec((1, tm, D_MODEL), lambda b, i: (b, i, 0))
    return pl.pallas_call(
        _odd_out_kernel,
        grid=(bsz, t_len // tm),
        in_specs=[main(0), main(1), main(2), main(3), prev(1), prev(2), nxt(1), nxt(2),
                  pl.BlockSpec(conv_w.shape, lambda b, i: (0, 0)),
                  pl.BlockSpec((1, C_WIDTH), lambda b, i: (0, 0)),
                  xspec,
                  pl.BlockSpec((1, 1, 3 * D_MODEL), lambda b, i: (mod_base + mod_stride * b, 0, 0)),
                  pl.BlockSpec((C_WIDTH, D_MODEL), lambda b, i: (0, 0))],
        out_specs=xspec,
        out_shape=jax.ShapeDtypeStruct(x.shape, F32),
        compiler_params=_params("parallel", "parallel"),
        name="odd_out",
    )(proj, proj, proj, proj, proj, proj, proj, proj, conv_w, conv_b, x, mods, w_out)


def _rope_tables(t_len):
    pos = jnp.arange(t_len, dtype=jnp.int32)
    row = (pos // GRID_W).astype(F32)
    col = (pos % GRID_W).astype(F32)
    half = B_HD // 2
    inv = ROPE_BASE ** (-jnp.arange(0, half, 2, dtype=F32) / half)
    ar = row[:, None] * inv
    ac = col[:, None] * inv
    cos = jnp.concatenate([jnp.cos(ar), jnp.cos(ar), jnp.cos(ac), jnp.cos(ac)], axis=-1)
    sin = jnp.concatenate([-jnp.sin(ar), jnp.sin(ar), -jnp.sin(ac), jnp.sin(ac)], axis=-1)
    return jnp.tile(cos, (1, 2)), jnp.tile(sin, (1, 2))


def _proj_rows(x, norm_w, mods, mod_base, mod_stride, w, flatten):
    bsz, t_len, _ = x.shape
    if flatten:
        x = x.reshape(1, bsz * t_len, D_MODEL)
    rows = x.shape[1]
    tm = next(c for c in (2048, 1024, 512, 256, 128) if rows % c == 0)
    out = _in_proj(x, norm_w, mods, mod_base, mod_stride, w, tm)
    return out.reshape(bsz, t_len, -1)


def _even_layer(x, mods, mod_base, mod_stride, flatten, norm_w, w_in, w_out, rw, dp, lam_init,
                s0, ctx_k, ctx_v, rope, emit_cache):
    bsz, t_len, _ = x.shape
    proj = _proj_rows(x, norm_w, mods, mod_base, mod_stride, w_in, flatten)
    mu_rkv, mu_lo, w0, w_up, a0, a_up, k_k, k_a, r_k, lnx_w, lnx_b = rw
    q_norm2, k_norm2, lam_vec, subln = dp
    tm = min(t_len, 256)
    cos, sin = rope if rope is not None else (None, None)
    prep = _qk_prep(proj, q_norm2, k_norm2, cos, sin, tm, emit_cache)
    qn, kn = prep[:2]
    yb = _attention(lam_vec, qn, kn, ctx_k, ctx_v, proj, subln, lam_init, min(t_len, 1024))
    scan = lambda d, fwd: _rwkv(proj, s0[d], mu_rkv, mu_lo, w0[d:d + 1], w_up[d], a0[d:d + 1], a_up[d],
                                k_k, k_a, r_k, forward=fwd)
    o_f, s_f = scan(0, None)
    o, bon, s_b = scan(1, (a0[0:1], a_up[0], o_f))
    s_new = jnp.stack([s_f, s_b], axis=0)
    x_new = _even_out(o, bon, proj, yb, x, mods, mod_base, mod_stride, lnx_w, lnx_b, w_out, min(t_len, 512))
    k_cache = prep[2] if emit_cache else None
    v_cache = proj[:, :, 6 * D_MODEL:7 * D_MODEL].astype(F32) if emit_cache else None
    return x_new, s_new, k_cache, v_cache


def _odd_layer(x, mods, mod_base, mod_stride, flatten, norm_w, w_in, conv_w, conv_b, w_out):
    proj = _proj_rows(x, norm_w, mods, mod_base, mod_stride, w_in, flatten)
    tm = min(x.shape[1], 512)
    return _odd_out(proj, x, mods, mod_base, mod_stride, conv_w, conv_b, w_out, tm)


def kernel(x_prompt, x_sample, state_rwkv_fwd, state_rwkv_bwd, cache_diff_k, cache_diff_v, c, c_ctx,
           norm_w, ada_w, ada_b,
           e_w_in, e_w_out, e_mu, e_w0, e_w_up, e_a0, e_a_up, e_k_k, e_k_a, e_r_k, e_lnx_w, e_lnx_b,
           e_q_norm, e_k_norm, e_lambda, e_subln,
           o_w_in, o_conv_w, o_conv_b, o_w_out):
    depth = norm_w.shape[0]
    bsz, seq = x_prompt.shape[:2]
    dec_b, dec_t = x_sample.shape[:2]
    past = cache_diff_k.shape[2]
    assert dec_b + 1 <= MOD_ROWS
    cvec = jnp.concatenate([c_ctx[None, :], c, jnp.zeros((MOD_ROWS - 1 - dec_b, D_MODEL), F32)], axis=0)
    mods = _ada(cvec, ada_w, ada_b)
    rope = _rope_tables(dec_t)
    rkv_w = 3 * A_WIDTH
    lora_end = rkv_w + 4 * LORA

    xp, xs = x_prompt, x_sample
    new_sf, new_sb, new_k, new_v = [], [], [], []
    for layer in range(depth):
        i = layer // 2
        nw = norm_w[layer][None, :]
        base = layer * MOD_ROWS
        if layer % 2 == 0:
            lam_init = 0.8 - 0.6 * math.exp(-0.3 * layer)
            w_in = jnp.concatenate([e_w_in[i][:, :rkv_w], e_w_in[i][:, lora_end:], e_w_in[i][:, rkv_w:lora_end]],
                                   axis=1).astype(BF16)
            w_out = e_w_out[i].astype(BF16)
            row = lambda t: t.reshape(1, -1)
            rw = (row(e_mu[i][:rkv_w]), row(e_mu[i][rkv_w:lora_end]), e_w0[i], e_w_up[i], e_a0[i], e_a_up[i],
                  row(e_k_k[i]), row(e_k_a[i]), row(e_r_k[i]), row(e_lnx_w[i]), row(e_lnx_b[i]))
            dp = (jnp.tile(e_q_norm[i], 2)[None, :], jnp.tile(e_k_norm[i], 2)[None, :], e_lambda[i],
                  row(e_subln[i]))
            zeros = jnp.zeros((2, bsz) + _STATE_BD, F32)
            xp, s_p, k_c, v_c = _even_layer(xp, mods, base, 0, True, nw, w_in, w_out, rw, dp, lam_init,
                                            zeros, None, None, None, True)
            s_p = _state_from_bd(s_p)
            s0 = _state_to_bd(jnp.stack([state_rwkv_fwd[:, i], state_rwkv_bwd[:, i]], axis=0))
            ctx_k = cache_diff_k[:, i].reshape(dec_b, past, B_WIDTH).astype(BF16)
            ctx_v = cache_diff_v[:, i].reshape(dec_b, past, B_WIDTH).astype(BF16)
            xs, _, _, _ = _even_layer(xs, mods, base + 1, 1, False, nw, w_in, w_out, rw, dp, lam_init,
                                      s0, ctx_k, ctx_v, rope, False)
            new_sf.append(s_p[0])
            new_sb.append(s_p[1])
            new_k.append(k_c.reshape(bsz, seq, B_HEADS, 2, B_HD))
            new_v.append(v_c.reshape(bsz, seq, B_HEADS, 2 * B_HD))
        else:
            w_in = o_w_in[i].astype(BF16)
            w_out = o_w_out[i].astype(BF16)
            cb = o_conv_b[i][None, :]
            xp = _odd_layer(xp, mods, base, 0, True, nw, w_in, o_conv_w[i], cb, w_out)
            xs = _odd_layer(xs, mods, base + 1, 1, False, nw, w_in, o_conv_w[i], cb, w_out)
    return (xp, xs, jnp.stack(new_sf, axis=1), jnp.stack(new_sb, axis=1),
            jnp.stack(new_k, axis=1), jnp.stack(new_v, axis=1))
```

```python
import functools
import math

import jax
import jax.numpy as jnp
from jax import lax
from jax.experimental import pallas as pl
from jax.experimental.pallas import tpu as pltpu

F32 = jnp.float32
BF16 = jnp.bfloat16
HI = lax.Precision.HIGHEST
PH = None

D_MODEL = 1024
A_WIDTH = 1024
A_HEAD = 64
A_HEADS = A_WIDTH // A_HEAD
LORA = 64
B_HD = 64
B_HEADS = 8
B_WIDTH = 1024
C_WIDTH = 2048
GRID_W = 64
ROPE_BASE = 10000.0
NORM_EPS = 1e-6
GN_EPS = 64e-5
MOD_ROWS = 8
CHUNK = 64
RWKV_SUB = 4
SUBLANES = 8
HALO = 16
VMEM_LIMIT = 48 * 1024 * 1024
ATTN_ROWS = 128
ATTN_SMALL = 256 * 256

_NT = (((1,), (1,)), ((), ()))
_TN = (((0,), (0,)), ((), ()))


def _dot(x, y, dims=None, precision=None):
    if dims is None:
        return jnp.dot(x, y, precision=precision, preferred_element_type=F32)
    return lax.dot_general(x, y, dims, precision=precision, preferred_element_type=F32)


def _params(*sem):
    return pltpu.CompilerParams(dimension_semantics=sem, vmem_limit_bytes=VMEM_LIMIT)


def _ada_kernel(c_ref, w_ref, b_ref, o_ref):
    cv = c_ref[...]
    s = cv * jax.nn.sigmoid(cv)
    o_ref[0] = _dot(s, w_ref[0], precision=HI) + b_ref[0]


def _ada(cvec, ada_w, ada_b):
    depth = ada_w.shape[0]
    tn = D_MODEL
    out = pl.pallas_call(
        _ada_kernel,
        grid=(depth, 3 * D_MODEL // tn),
        in_specs=[pl.BlockSpec((MOD_ROWS, D_MODEL), lambda l, j: (0, 0)),
                  pl.BlockSpec((1, D_MODEL, tn), lambda l, j: (l, 0, j)),
                  pl.BlockSpec((1, 1, tn), lambda l, j: (l, 0, j))],
        out_specs=pl.BlockSpec((1, MOD_ROWS, tn), lambda l, j: (l, 0, j)),
        out_shape=jax.ShapeDtypeStruct((depth, MOD_ROWS, 3 * D_MODEL), F32),
        compiler_params=_params("parallel", "parallel"),
        name="ada",
    )(cvec, ada_w, ada_b.reshape(depth, 1, 3 * D_MODEL))
    return out.reshape(depth * MOD_ROWS, 1, 3 * D_MODEL)


def _in_proj_kernel(x_ref, nw_ref, mod_ref, w_ref, o_ref, h_ref):
    @pl.when(pl.program_id(2) == 0)
    def _():
        x = x_ref[0]
        ms = jnp.mean(x * x, axis=-1, keepdims=True)
        y = x * lax.rsqrt(ms + NORM_EPS) * nw_ref[...]
        m = mod_ref[0]
        h = y * (1.0 + m[:, D_MODEL:2 * D_MODEL]) + m[:, :D_MODEL]
        h_ref[...] = h.astype(BF16)

    o_ref[0] = _dot(h_ref[...], w_ref[...]).astype(o_ref.dtype)


def _in_proj(x, norm_w, mods, mod_base, mod_stride, w, tm):
    bsz, t_len, _ = x.shape
    n_out = w.shape[1]
    tn = next(c for c in (1024, 768, 512, 256, 128) if n_out % c == 0)
    return pl.pallas_call(
        _in_proj_kernel,
        grid=(bsz, t_len // tm, n_out // tn),
        in_specs=[pl.BlockSpec((1, tm, D_MODEL), lambda b, i, j: (b, i, 0)),
                  pl.BlockSpec((1, D_MODEL), lambda b, i, j: (0, 0)),
                  pl.BlockSpec((1, 1, 3 * D_MODEL), lambda b, i, j: (mod_base + mod_stride * b, 0, 0)),
                  pl.BlockSpec((D_MODEL, tn), lambda b, i, j: (0, j))],
        out_specs=pl.BlockSpec((1, tm, tn), lambda b, i, j: (b, i, j)),
        out_shape=jax.ShapeDtypeStruct((bsz, t_len, n_out), BF16),
        scratch_shapes=[pltpu.VMEM((tm, D_MODEL), BF16)],
        compiler_params=_params("parallel", "parallel", "arbitrary"),
        name="in_proj",
    )(x, norm_w, mods, w)


def _qk_prep_kernel(*refs, rope, emit_cache):
    pq_ref, pk_ref, qn_ref, kn_ref = refs[:4]
    pos = 4
    if rope:
        cos_ref, sin_ref = refs[pos:pos + 2]
        pos += 2
    q_out, k_out = refs[pos:pos + 2]
    kc_out = refs[pos + 2] if emit_cache else None
    width = 2 * B_HD
    tm = pq_ref.shape[1]
    lane = lax.broadcasted_iota(jnp.int32, (tm, width), 1)
    first16 = (lane % 32) < 16

    grp = lax.broadcasted_iota(jnp.int32, (width, width), 0) // B_HD
    same = jnp.where(grp == lax.broadcasted_iota(jnp.int32, (width, width), 1) // B_HD, 1.0, 0.0).astype(BF16)

    def norm(x, w):
        ss = x * x
        hi = ss.astype(BF16)
        lo = (ss - hi.astype(F32)).astype(BF16)
        ms = (_dot(hi, same) + _dot(lo, same)) * (1.0 / B_HD)
        return x * lax.rsqrt(ms + NORM_EPS) * w

    def rot(x):
        if not rope:
            return x
        partner = jnp.where(first16, pltpu.roll(x, width - 16, 1), pltpu.roll(x, 16, 1))
        return x * cos_ref[...] + partner * sin_ref[...]

    for h in range(B_HEADS):
        sl = slice(h * width, (h + 1) * width)
        q = norm(pq_ref[0, :, sl].astype(F32), qn_ref[...])
        k = norm(pk_ref[0, :, sl].astype(F32), kn_ref[...])
        if emit_cache:
            kc_out[0, :, sl] = k
        q_out[0, :, sl] = (rot(q) * (B_HD ** -0.5 * math.log2(math.e))).astype(BF16)
        k_out[0, :, sl] = rot(k).astype(BF16)


def _qk_prep(proj, q_norm2, k_norm2, cos, sin, tm, emit_cache):
    bsz, t_len, _ = proj.shape
    rope = cos is not None
    col = lambda c: pl.BlockSpec((1, tm, B_WIDTH), lambda b, i: (b, i, c))
    vec = pl.BlockSpec((1, 2 * B_HD), lambda b, i: (0, 0))
    in_specs = [col(4), col(5), vec, vec]
    args = [proj, proj, q_norm2, k_norm2]
    if rope:
        tab = pl.BlockSpec((tm, 2 * B_HD), lambda b, i: (i, 0))
        in_specs += [tab, tab]
        args += [cos, sin]
    out_spec = pl.BlockSpec((1, tm, B_WIDTH), lambda b, i: (b, i, 0))
    out_shape = [jax.ShapeDtypeStruct((bsz, t_len, B_WIDTH), BF16)] * 2
    out_specs = [out_spec] * 2
    if emit_cache:
        out_shape = out_shape + [jax.ShapeDtypeStruct((bsz, t_len, B_WIDTH), F32)]
        out_specs = out_specs + [out_spec]
    return pl.pallas_call(
        functools.partial(_qk_prep_kernel, rope=rope, emit_cache=emit_cache),
        grid=(bsz, t_len // tm),
        in_specs=in_specs, out_specs=out_specs, out_shape=out_shape,
        compiler_params=_params("parallel", "parallel"),
        name="qk_prep",
    )(*args)


def _attn_kernel(*refs, lam_init, has_ctx):
    lam_ref, q_ref, k_ref, v_ref = refs[:4]
    pos = 4
    if has_ctx:
        ck_ref, cv_ref = refs[pos:pos + 2]
        pos += 2
    gb_ref, sub_ref, o_ref, k_s, v_s = refs[pos:pos + 5]
    width = 2 * B_HD
    t_len = k_ref.shape[1]
    tq = q_ref.shape[1]

    heads = q_ref.shape[2] // width

    @pl.when(pl.program_id(2) == 0)
    def _():
        for h in range(heads):
            sl = slice(h * width, (h + 1) * width)
            k_s[h, 0:t_len, :] = k_ref[0, :, sl]
            v_s[h, 0:t_len, 0:width] = v_ref[0, :, sl]
            if has_ctx:
                k_s[h, t_len:, :] = ck_ref[0, :, sl]
                v_s[h, t_len:, 0:width] = cv_ref[0, :, sl]
        v_s[:, :, width:] = jnp.ones(v_s.shape[:2] + (width,), BF16)

    lv = lam_ref[...]
    lam = (jnp.exp(jnp.sum(lv[0:1] * lv[1:2], axis=-1, keepdims=True))
           - jnp.exp(jnp.sum(lv[2:3] * lv[3:4], axis=-1, keepdims=True)) + lam_init)
    lane = lax.broadcasted_iota(jnp.int32, (tq, width), 1)
    n_groups = 2 * tq // ATTN_ROWS

    for h in range(heads):
        sl = slice(h * width, (h + 1) * width)
        q = q_ref[0, :, sl]
        zero = jnp.zeros_like(q)
        q2 = jnp.concatenate([jnp.where(lane < B_HD, q, zero), jnp.where(lane < B_HD, zero, q)], axis=0)
        scores = lambda g: _dot(q2[g * ATTN_ROWS:(g + 1) * ATTN_ROWS], k_s[h], _NT)

        def softmax_pv(s):
            m = jnp.max(s, axis=-1, keepdims=True)
            acc = _dot(jnp.exp2(s - m).astype(BF16), v_s[h])
            return acc[:, :width] / acc[:, width:]

        parts = []
        s_next = scores(0)
        for g in range(n_groups):
            s_cur = s_next
            if g + 1 < n_groups:
                s_next = scores(g + 1)
            parts.append(softmax_pv(s_cur))
        comp = jnp.concatenate(parts, axis=0)
        o = comp[:tq] - lam * comp[tq:]
        ms = jnp.mean(o * o, axis=-1, keepdims=True)
        o = o * lax.rsqrt(ms + NORM_EPS) * sub_ref[...] * (1.0 - lam_init)
        gb = gb_ref[0, :, sl].astype(F32)
        o_ref[0, :, sl] = (o * (gb * jax.nn.sigmoid(gb))).astype(BF16)


def _attention(lam_vec, q, k, ctx_k, ctx_v, proj, subln, lam_init, tq):
    bsz, t_len, _ = q.shape
    has_ctx = ctx_k is not None
    width = 2 * B_HD
    n_keys = t_len + (ctx_k.shape[1] if has_ctx else 0)
    hp = B_HEADS if n_keys * tq <= ATTN_SMALL else 1
    span = hp * width
    per = B_HEADS // hp
    qspec = pl.BlockSpec((1, tq, span), lambda b, h, i: (b, i, h))
    kspec = pl.BlockSpec((1, t_len, span), lambda b, h, i: (b, 0, h))
    vspec = pl.BlockSpec((1, t_len, span), lambda b, h, i: (b, 0, 6 * per + h))
    in_specs = [pl.BlockSpec((4, B_HD), lambda b, h, i: (0, 0)), qspec, kspec, vspec]
    args = [lam_vec, q, k, proj]
    if has_ctx:
        cspec = pl.BlockSpec((1, ctx_k.shape[1], span), lambda b, h, i: (b, 0, h))
        in_specs += [cspec, cspec]
        args += [ctx_k, ctx_v]
    in_specs += [pl.BlockSpec((1, tq, span), lambda b, h, i: (b, i, 7 * per + h)),
                 pl.BlockSpec((1, width), lambda b, h, i: (0, 0))]
    args += [proj, subln]
    return pl.pallas_call(
        functools.partial(_attn_kernel, lam_init=lam_init, has_ctx=has_ctx),
        grid=(bsz, per, t_len // tq),
        in_specs=in_specs, out_specs=qspec,
        out_shape=jax.ShapeDtypeStruct((bsz, t_len, B_WIDTH), BF16),
        scratch_shapes=[pltpu.VMEM((hp, n_keys, width), BF16), pltpu.VMEM((hp, n_keys, 2 * width), BF16)],
        compiler_params=_params("parallel", "parallel", "arbitrary"),
        name="diff_attn",
    )(*args)


def _rwkv_kernel(*refs, n_steps, n_sub, reverse):
    (rkv_ref, rkv_p_ref, rkv_n_ref, lo_ref, lo_p_ref, lo_n_ref, mu_rkv_ref, mu_lo_ref,
     w0_ref, wup_ref, a0_ref, aup_ref, kk_ref, ka_ref, rk_ref, s0_ref) = refs[:16]
    if reverse:
        a0f_ref, aupf_ref, of_ref, o_ref, bon_ref, s_ref, xpad_s, st_s = refs[16:]
    else:
        o_ref, s_ref, xpad_s, st_s = refs[16:]
    C = CHUNK
    R = n_sub * C
    PAD = SUBLANES
    c = pl.program_id(1)
    blk = n_steps - 1 - c if reverse else c
    first = blk == 0
    last = blk == n_steps - 1
    row = lax.broadcasted_iota(jnp.int32, (R, 1), 0)
    up = lambda ref: ref[0].astype(F32)

    block = {}

    def block_values():
        if block:
            return block["logw"], block["a"]
        lo = up(lo_ref)
        lo_prev = jnp.where(first, 0.0, up(lo_p_ref)[HALO - 1:HALO, :])
        lo_next = jnp.where(last, 0.0, up(lo_n_ref)[0:1, :])
        lo_p = jnp.where(row == 0, lo_prev, pltpu.roll(lo, 1, 0))
        lo_n = jnp.where(row == R - 1, lo_next, pltpu.roll(lo, R - 1, 0))
        lo = lo + mu_lo_ref[...] * (0.5 * (lo_p + lo_n) - lo)
        dcol = 1 if reverse else 0
        wl = lo[:, dcol * LORA:(dcol + 1) * LORA]
        al = lo[:, (2 + dcol) * LORA:(3 + dcol) * LORA]
        xw = w0_ref[...] + _dot(jnp.tanh(wl), wup_ref[...], precision=PH)
        block["logw"] = -math.exp(-0.5) * jax.nn.sigmoid(xw)
        block["a"] = jax.nn.sigmoid(a0_ref[...] + _dot(al, aup_ref[...], precision=PH))
        if reverse:
            al_f = lo[:, 2 * LORA:3 * LORA]
            block["a_sum"] = block["a"] + jax.nn.sigmoid(a0f_ref[...] + _dot(al_f, aupf_ref[...], precision=PH))
        xpad_s[PAD:PAD + R, :] = up(rkv_ref)
        xpad_s[PAD - 1:PAD, :] = jnp.where(first, 0.0, up(rkv_p_ref)[HALO - 1:HALO, :])
        xpad_s[PAD + R:PAD + R + 1, :] = jnp.where(last, 0.0, up(rkv_n_ref)[0:1, :])
        return block["logw"], block["a"]

    ti = lax.broadcasted_iota(jnp.int32, (C, C), 0)
    si = lax.broadcasted_iota(jnp.int32, (C, C), 1)
    incl = (si >= ti) if reverse else (si <= ti)
    tri = jnp.where(incl, 1.0, 0.0).astype(BF16)

    @pl.when(c == 0)
    def _():
        st_s[...] = s0_ref[0]

    W2 = 2 * A_HEAD
    lane = lax.broadcasted_iota(jnp.int32, (C, W2), 1)
    trow = lax.broadcasted_iota(jnp.int32, (C, W2), 0)
    low = lane < A_HEAD
    before2 = (lane % A_HEAD - trow) if reverse else (trow - lane % A_HEAD)
    strict2 = before2 > 0
    incl2 = before2 >= 0
    eye2 = jnp.where(before2 == 0, 1.0, 0.0)
    brow = lax.broadcasted_iota(jnp.int32, (W2, W2), 0)
    bcol = lax.broadcasted_iota(jnp.int32, (W2, W2), 1)
    bdmask = (brow < A_HEAD) == (bcol < A_HEAD)
    bdmask2 = jnp.concatenate([bdmask, bdmask], axis=1)
    eye_bd = jnp.where(brow == bcol, 1.0, 0.0)

    def bd(x):
        return jnp.where(bdmask if x.shape[1] == W2 else bdmask2, jnp.concatenate([x, x], axis=0), 0.0)

    H = C // 2
    lane_h = lax.broadcasted_iota(jnp.int32, (H, W2), 1)
    near = (lane_h % A_HEAD) < H
    eye4 = jnp.where(lax.broadcasted_iota(jnp.int32, (H, W2), 0) == lane_h % H, 1.0, 0.0)
    mask4 = brow // H == bcol // H
    shift4 = (brow // H + 1 == bcol // H) if reverse else (brow // H == bcol // H + 1)

    def bd4(x, mask=mask4):
        return jnp.where(mask, jnp.concatenate([x] * 4, axis=0), 0.0)

    def head_sum(x):
        s0 = jnp.sum(jnp.where(low, x, 0.0), axis=-1, keepdims=True)
        s1 = jnp.sum(jnp.where(low, 0.0, x), axis=-1, keepdims=True)
        return jnp.where(low, s0, s1)

    def rows(*xs):
        return jnp.concatenate(xs, axis=0)

    n_pairs = A_HEADS // 2
    state = [st_s[p] for p in range(n_pairs)]

    def prepare(j, p):
        rs = slice(j * C, (j + 1) * C)
        sl = slice(p * W2, (p + 1) * W2)
        logw, a = block_values()

        def shifted(col0):
            cs = slice(col0 + p * W2, col0 + (p + 1) * W2)
            base = PAD + j * C
            x = xpad_s[base:base + C, cs]
            nb = xpad_s[base - 1:base - 1 + C, cs] + xpad_s[base + 1:base + 1 + C, cs]
            return x + mu_rkv_ref[:, cs] * (0.5 * nb - x)

        r = shifted(0)
        k = shifted(A_WIDTH)
        vm = shifted(2 * A_WIDTH)
        lw = logw[rs, sl]
        av = a[rs, sl]
        kd = k * (1.0 + (av - 1.0) * ka_ref[:, sl])
        kk = k * kk_ref[:, sl]
        kk = kk / jnp.maximum(jnp.sqrt(head_sum(kk * kk)), 1e-12)
        if reverse:
            k_both = k * (2.0 + (block["a_sum"][rs, sl] - 2.0) * ka_ref[:, sl])
            bon_ref[0, rs, sl] = head_sum(r * k_both * rk_ref[:, sl]) * vm
        t0 = lw.astype(BF16)
        e0 = lw - t0.astype(F32)
        t1 = e0.astype(BF16)
        t2 = (e0 - t1.astype(F32)).astype(BF16)
        cum = _dot(tri, t0) + _dot(tri, t1) + _dot(tri, t2)
        pinv = jnp.exp(-cum)
        pt = jnp.exp(jnp.sum(lw, axis=0, keepdims=True))
        return kk * jnp.exp(cum - lw), kk * av * pinv, kd * pinv, r * jnp.exp(cum), vm, pt

    def solve(j, p, am, bm, kp, rm, vm, pt):
        rs = slice(j * C, (j + 1) * C)
        sl = slice(p * W2, (p + 1) * W2)
        ar = rows(am, rm)
        xb = _dot(ar, bd(bm), _NT, PH)
        xk = _dot(ar, bd(kp), _NT, PH)
        yield
        nab = jnp.where(strict2, -xb[:C], 0.0)
        mak = jnp.where(strict2, xk[:C], 0.0)
        mrb = jnp.where(incl2, xb[C:], 0.0)
        mrk = jnp.where(incl2, xk[C:], 0.0)
        np4 = jnp.where(near, nab[:H], nab[H:])
        t4 = eye4 + np4
        pw = _dot(np4, bd4(np4), precision=PH)
        mv = _dot(rows(mak, mrk), bd(vm), precision=PH)
        yield
        for _ in range(int(math.log2(H)) - 2):
            both = _dot(rows(t4, pw), bd4(pw), precision=PH)
            t4 = t4 + both[:H]
            pw = both[H:]
            yield
        t4 = t4 + _dot(t4, bd4(pw), precision=PH)
        yield
        n_off = jnp.where(near, 0.0, nab[:H]) if reverse else jnp.where(near, nab[H:], 0.0)
        x_off = _dot(n_off, bd4(t4), precision=PH)
        yield
        t_off = _dot(t4, bd4(x_off, shift4), precision=PH)
        if reverse:
            tinv = rows(jnp.where(near, t4, t_off), jnp.where(near, 0.0, t4))
        else:
            tinv = rows(jnp.where(near, t4, 0.0), jnp.where(near, t_off, t4))
        yield
        wu = jnp.concatenate([_dot(tinv, bd(am), precision=PH), _dot(tinv, bd(mv[:C]), precision=PH)], axis=1)
        yield
        mw = _dot(mrb, bd(wu), precision=PH)
        bwu = _dot(bm * pt, wu, _TN, PH)
        kv = _dot(kp * pt, vm, _TN, PH)
        yield
        rp = rm - mw[:, :W2]
        o0 = mv[C:] - mw[:, W2:]
        gt = eye_bd * pt - jnp.where(bdmask, bwu[:, :W2], 0.0)
        ht = jnp.where(bdmask, kv - bwu[:, W2:], 0.0)
        oc = _dot(rows(rp, gt), state[p], precision=PH)
        o_ref[0, rs, sl] = (oc[:C] + o0 + of_ref[0, rs, sl]) if reverse else (oc[:C] + o0)
        state[p] = oc[C:] + ht

    order = range(n_sub - 1, -1, -1) if reverse else range(n_sub)
    items = [(j, p) for j in order for p in range(n_pairs)]
    operands = [prepare(j, p) for j, p in items]
    active = [solve(j, p, *ops) for (j, p), ops in zip(items, operands)]
    while active:
        active = [gen for gen in active if next(gen, "done") != "done"]
    for p in range(n_pairs):
        st_s[p] = state[p]

    @pl.when(c == n_steps - 1)
    def _():
        s_ref[0] = st_s[...]


_STATE_BD = (A_HEADS // 2, 2 * A_HEAD, 2 * A_HEAD)


def _state_to_bd(s):
    lead = s.shape[:-3]
    st = jnp.swapaxes(s, -1, -2).reshape(lead + (A_HEADS // 2, 2, A_HEAD, A_HEAD))
    z = jnp.zeros_like(st[..., 0, :, :])
    top = jnp.concatenate([st[..., 0, :, :], z], axis=-1)
    bot = jnp.concatenate([z, st[..., 1, :, :]], axis=-1)
    return jnp.concatenate([top, bot], axis=-2)


def _state_from_bd(st):
    pair = jnp.stack([st[..., :A_HEAD, :A_HEAD], st[..., A_HEAD:, A_HEAD:]], axis=-3)
    return jnp.swapaxes(pair.reshape(st.shape[:-3] + (A_HEADS, A_HEAD, A_HEAD)), -1, -2)


def _rwkv(proj, s0, mu_rkv, mu_lo, w0, w_up, a0, a_up, k_k, k_a, r_k, forward=None):
    reverse = forward is not None
    bsz, t_len, _ = proj.shape
    n_sub = next(n for n in (RWKV_SUB, 2, 1) if t_len % (n * CHUNK) == 0)
    rows = n_sub * CHUNK
    n_steps = t_len // rows
    n_halo = t_len // HALO
    per = rows // HALO
    blk = (lambda c: n_steps - 1 - c) if reverse else (lambda c: c)
    rkv_w = 3 * A_WIDTH
    lo_w = 4 * LORA
    lo_col = (8 * D_MODEL) // lo_w

    def main(width, col):
        return pl.BlockSpec((1, rows, width), lambda b, c: (b, blk(c), col))

    def prev(width, col):
        return pl.BlockSpec((1, HALO, width), lambda b, c: (b, jnp.maximum(blk(c) * per - 1, 0), col))

    def nxt(width, col):
        return pl.BlockSpec((1, HALO, width), lambda b, c: (b, jnp.minimum((blk(c) + 1) * per, n_halo - 1), col))

    vec = lambda width: pl.BlockSpec((1, width), lambda b, c: (0, 0))
    mat = pl.BlockSpec((LORA, A_WIDTH), lambda b, c: (0, 0))
    sspec = pl.BlockSpec((1,) + _STATE_BD, lambda b, c: (b, 0, 0, 0))
    ospec = pl.BlockSpec((1, rows, A_WIDTH), lambda b, c: (b, blk(c), 0))
    big = jax.ShapeDtypeStruct((bsz, t_len, A_WIDTH), F32)
    state = jax.ShapeDtypeStruct((bsz,) + _STATE_BD, F32)
    in_specs = [main(rkv_w, 0), prev(rkv_w, 0), nxt(rkv_w, 0),
                main(lo_w, lo_col), prev(lo_w, lo_col), nxt(lo_w, lo_col),
                vec(rkv_w), vec(lo_w), vec(A_WIDTH), mat, vec(A_WIDTH), mat,
                vec(A_WIDTH), vec(A_WIDTH), vec(A_WIDTH), sspec]
    args = [proj, proj, proj, proj, proj, proj, mu_rkv, mu_lo, w0, w_up, a0, a_up, k_k, k_a, r_k, s0]
    if reverse:
        in_specs += [vec(A_WIDTH), mat, ospec]
        args += list(forward)
    return pl.pallas_call(
        functools.partial(_rwkv_kernel, n_steps=n_steps, n_sub=n_sub, reverse=reverse),
        grid=(bsz, n_steps),
        in_specs=in_specs,
        out_specs=[ospec, ospec, sspec] if reverse else [ospec, sspec],
        out_shape=[big, big, state] if reverse else [big, state],
        scratch_shapes=[pltpu.VMEM((rows + 2 * SUBLANES, rkv_w), F32), pltpu.VMEM(_STATE_BD, F32)],
        compiler_params=_params("parallel", "arbitrary"),
        name="rwkv_bwd" if reverse else "rwkv_fwd",
    )(*args)


def _even_out_kernel(oa_ref, bon_ref, ga_ref, yb_ref, x_ref, mod_ref, lw_ref, lb_ref, w_ref, o_ref, y_s):
    o = oa_ref[0]
    bon = bon_ref[0]
    ga = ga_ref[0].astype(F32)
    gate_a = ga * jax.nn.sigmoid(ga)
    W2 = 2 * A_HEAD
    low = lax.broadcasted_iota(jnp.int32, (o.shape[0], W2), 1) < A_HEAD

    def head_mean(x):
        s0 = jnp.sum(jnp.where(low, x, 0.0), axis=-1, keepdims=True)
        s1 = jnp.sum(jnp.where(low, 0.0, x), axis=-1, keepdims=True)
        return jnp.where(low, s0, s1) * (1.0 / A_HEAD)

    sls = [slice(p * W2, (p + 1) * W2) for p in range(A_HEADS // 2)]
    cen = [o[:, sl] for sl in sls]
    cen = [x - head_mean(x) for x in cen]
    var = [head_mean(x * x) for x in cen]
    for sl, x, vr in zip(sls, cen, var):
        yh = x * lax.rsqrt(vr + GN_EPS) * lw_ref[:, sl] + lb_ref[:, sl]
        y_s[:, sl] = ((yh + bon[:, sl]) * gate_a[:, sl]).astype(BF16)
    y_s[:, A_WIDTH:] = yb_ref[0]
    out = _dot(y_s[...], w_ref[...])
    gate = mod_ref[0][:, 2 * D_MODEL:]
    o_ref[0] = x_ref[0] + gate * out


def _even_out(o, bon, proj, yb, x, mods, mod_base, mod_stride, lnx_w, lnx_b, w_out, tm):
    bsz, t_len, _ = x.shape
    row = lambda col: pl.BlockSpec((1, tm, D_MODEL), lambda b, i: (b, i, col))
    vec = pl.BlockSpec((1, A_WIDTH), lambda b, i: (0, 0))
    return pl.pallas_call(
        _even_out_kernel,
        grid=(bsz, t_len // tm),
        in_specs=[row(0), row(0), row(3), row(0), row(0),
                  pl.BlockSpec((1, 1, 3 * D_MODEL), lambda b, i: (mod_base + mod_stride * b, 0, 0)),
                  vec, vec, pl.BlockSpec((A_WIDTH + B_WIDTH, D_MODEL), lambda b, i: (0, 0))],
        out_specs=row(0),
        out_shape=jax.ShapeDtypeStruct(x.shape, F32),
        scratch_shapes=[pltpu.VMEM((tm, A_WIDTH + B_WIDTH), BF16)],
        compiler_params=_params("parallel", "parallel"),
        name="even_out",
    )(o, bon, proj, yb, x, mods, lnx_w, lnx_b, w_out)


def _odd_out_kernel(bg_ref, cg_ref, u_ref, z_ref, cgp_ref, up_ref, cgn_ref, un_ref, cw_ref, cb_ref,
                    x_ref, mod_ref, w_ref, o_ref):
    i = pl.program_id(1)
    tm = bg_ref.shape[1]
    row = lax.broadcasted_iota(jnp.int32, (tm, 1), 0)
    f32 = lambda ref: ref[0].astype(F32)
    cu = f32(cg_ref) * f32(u_ref)
    prev_row = jnp.where(i == 0, 0.0, (f32(cgp_ref) * f32(up_ref))[HALO - 1:HALO, :])
    next_row = jnp.where(i == pl.num_programs(1) - 1, 0.0, (f32(cgn_ref) * f32(un_ref))[0:1, :])
    cu_prev = jnp.where(row == 0, prev_row, pltpu.roll(cu, 1, 0))
    cu_next = jnp.where(row == tm - 1, next_row, pltpu.roll(cu, tm - 1, 0))
    cw = cw_ref[...]
    conv = cu_prev * cw[0:1] + cu * cw[1:2] + cu_next * cw[2:3] + cb_ref[...]
    z = f32(z_ref)
    y = f32(bg_ref) * conv * (z * jax.nn.sigmoid(z))
    out = _dot(y.astype(BF16), w_ref[...])
    gate = mod_ref[0][:, 2 * D_MODEL:]
    o_ref[0] = x_ref[0] + gate * out


def _odd_out(proj, x, mods, mod_base, mod_stride, conv_w, conv_b, w_out, tm):
    bsz, t_len, _ = x.shape
    per = tm // HALO
    n_halo = t_len // HALO
    main = lambda col: pl.BlockSpec((1, tm, C_WIDTH), lambda b, i: (b, i, col))
    prev = lambda col: pl.BlockSpec((1, HALO, C_WIDTH), lambda b, i: (b, jnp.maximum(i * per - 1, 0), col))
    nxt = lambda col: pl.BlockSpec((1, HALO, C_WIDTH),
                                   lambda b, i: (b, jnp.minimum((i + 1) * per, n_halo - 1), col))
    xspec = pl.BlockSpec((1, tm, D_MODEL), lambda b, i: (b, i, 0))
    return pl.pallas_call(
        _odd_out_kernel,
        grid=(bsz, t_len // tm),
        in_specs=[main(0), main(1), main(2), main(3), prev(1), prev(2), nxt(1), nxt(2),
                  pl.BlockSpec(conv_w.shape, lambda b, i: (0, 0)),
                  pl.BlockSpec((1, C_WIDTH), lambda b, i: (0, 0)),
                  xspec,
                  pl.BlockSpec((1, 1, 3 * D_MODEL), lambda b, i: (mod_base + mod_stride * b, 0, 0)),
                  pl.BlockSpec((C_WIDTH, D_MODEL), lambda b, i: (0, 0))],
        out_specs=xspec,
        out_shape=jax.ShapeDtypeStruct(x.shape, F32),
        compiler_params=_params("parallel", "parallel"),
        name="odd_out",
    )(proj, proj, proj, proj, proj, proj, proj, proj, conv_w, conv_b, x, mods, w_out)


def _rope_tables(t_len):
    pos = jnp.arange(t_len, dtype=jnp.int32)
    row = (pos // GRID_W).astype(F32)
    col = (pos % GRID_W).astype(F32)
    half = B_HD // 2
    inv = ROPE_BASE ** (-jnp.arange(0, half, 2, dtype=F32) / half)
    ar = row[:, None] * inv
    ac = col[:, None] * inv
    cos = jnp.concatenate([jnp.cos(ar), jnp.cos(ar), jnp.cos(ac), jnp.cos(ac)], axis=-1)
    sin = jnp.concatenate([-jnp.sin(ar), jnp.sin(ar), -jnp.sin(ac), jnp.sin(ac)], axis=-1)
    return jnp.tile(cos, (1, 2)), jnp.tile(sin, (1, 2))


def _proj_rows(x, norm_w, mods, mod_base, mod_stride, w, flatten):
    bsz, t_len, _ = x.shape
    if flatten:
        x = x.reshape(1, bsz * t_len, D_MODEL)
    rows = x.shape[1]
    tm = next(c for c in (2048, 1024, 512, 256, 128) if rows % c == 0)
    out = _in_proj(x, norm_w, mods, mod_base, mod_stride, w, tm)
    return out.reshape(bsz, t_len, -1)


def _even_layer(x, mods, mod_base, mod_stride, flatten, norm_w, w_in, w_out, rw, dp, lam_init,
                s0, ctx_k, ctx_v, rope, emit_cache):
    bsz, t_len, _ = x.shape
    proj = _proj_rows(x, norm_w, mods, mod_base, mod_stride, w_in, flatten)
    mu_rkv, mu_lo, w0, w_up, a0, a_up, k_k, k_a, r_k, lnx_w, lnx_b = rw
    q_norm2, k_norm2, lam_vec, subln = dp
    tm = min(t_len, 512)
    cos, sin = rope if rope is not None else (None, None)
    prep = _qk_prep(proj, q_norm2, k_norm2, cos, sin, tm, emit_cache)
    qn, kn = prep[:2]
    yb = _attention(lam_vec, qn, kn, ctx_k, ctx_v, proj, subln, lam_init, min(t_len, 1024))
    scan = lambda d, fwd: _rwkv(proj, s0[d], mu_rkv, mu_lo, w0[d:d + 1], w_up[d], a0[d:d + 1], a_up[d],
                                k_k, k_a, r_k, forward=fwd)
    o_f, s_f = scan(0, None)
    o, bon, s_b = scan(1, (a0[0:1], a_up[0], o_f))
    s_new = jnp.stack([s_f, s_b], axis=0)
    x_new = _even_out(o, bon, proj, yb, x, mods, mod_base, mod_stride, lnx_w, lnx_b, w_out, min(t_len, 512))
    k_cache = prep[2] if emit_cache else None
    v_cache = proj[:, :, 6 * D_MODEL:7 * D_MODEL].astype(F32) if emit_cache else None
    return x_new, s_new, k_cache, v_cache


def _odd_layer(x, mods, mod_base, mod_stride, flatten, norm_w, w_in, conv_w, conv_b, w_out):
    proj = _proj_rows(x, norm_w, mods, mod_base, mod_stride, w_in, flatten)
    tm = min(x.shape[1], 512)
    return _odd_out(proj, x, mods, mod_base, mod_stride, conv_w, conv_b, w_out, tm)


def kernel(x_prompt, x_sample, state_rwkv_fwd, state_rwkv_bwd, cache_diff_k, cache_diff_v, c, c_ctx,
           norm_w, ada_w, ada_b,
           e_w_in, e_w_out, e_mu, e_w0, e_w_up, e_a0, e_a_up, e_k_k, e_k_a, e_r_k, e_lnx_w, e_lnx_b,
           e_q_norm, e_k_norm, e_lambda, e_subln,
           o_w_in, o_conv_w, o_conv_b, o_w_out):
    depth = norm_w.shape[0]
    bsz, seq = x_prompt.shape[:2]
    dec_b, dec_t = x_sample.shape[:2]
    past = cache_diff_k.shape[2]
    assert dec_b + 1 <= MOD_ROWS
    cvec = jnp.concatenate([c_ctx[None, :], c, jnp.zeros((MOD_ROWS - 1 - dec_b, D_MODEL), F32)], axis=0)
    mods = _ada(cvec, ada_w, ada_b)
    rope = _rope_tables(dec_t)
    rkv_w = 3 * A_WIDTH
    lora_end = rkv_w + 4 * LORA

    xp, xs = x_prompt, x_sample
    new_sf, new_sb, new_k, new_v = [], [], [], []
    for layer in range(depth):
        i = layer // 2
        nw = norm_w[layer][None, :]
        base = layer * MOD_ROWS
        if layer % 2 == 0:
            lam_init = 0.8 - 0.6 * math.exp(-0.3 * layer)
            w_in = jnp.concatenate([e_w_in[i][:, :rkv_w], e_w_in[i][:, lora_end:], e_w_in[i][:, rkv_w:lora_end]],
                                   axis=1).astype(BF16)
            w_out = e_w_out[i].astype(BF16)
            row = lambda t: t.reshape(1, -1)
            rw = (row(e_mu[i][:rkv_w]), row(e_mu[i][rkv_w:lora_end]), e_w0[i], e_w_up[i], e_a0[i], e_a_up[i],
                  row(e_k_k[i]), row(e_k_a[i]), row(e_r_k[i]), row(e_lnx_w[i]), row(e_lnx_b[i]))
            dp = (jnp.tile(e_q_norm[i], 2)[None, :], jnp.tile(e_k_norm[i], 2)[None, :], e_lambda[i],
                  row(e_subln[i]))
            zeros = jnp.zeros((2, bsz) + _STATE_BD, F32)
            xp, s_p, k_c, v_c = _even_layer(xp, mods, base, 0, True, nw, w_in, w_out, rw, dp, lam_init,
                                            zeros, None, None, None, True)
            s_p = _state_from_bd(s_p)
            s0 = _state_to_bd(jnp.stack([state_rwkv_fwd[:, i], state_rwkv_bwd[:, i]], axis=0))
            ctx_k = cache_diff_k[:, i].reshape(dec_b, past, B_WIDTH).astype(BF16)
            ctx_v = cache_diff_v[:, i].reshape(dec_b, past, B_WIDTH).astype(BF16)
            xs, _, _, _ = _even_layer(xs, mods, base + 1, 1, False, nw, w_in, w_out, rw, dp, lam_init,
                                      s0, ctx_k, ctx_v, rope, False)
            new_sf.append(s_p[0])
            new_sb.append(s_p[1])
            new_k.append(k_c.reshape(bsz, seq, B_HEADS, 2, B_HD))
            new_v.append(v_c.reshape(bsz, seq, B_HEADS, 2 * B_HD))
        else:
            w_in = o_w_in[i].astype(BF16)
            w_out = o_w_out[i].astype(BF16)
            cb = o_conv_b[i][None, :]
            xp = _odd_layer(xp, mods, base, 0, True, nw, w_in, o_conv_w[i], cb, w_out)
            xs = _odd_layer(xs, mods, base + 1, 1, False, nw, w_in, o_conv_w[i], cb, w_out)
    return (xp, xs, jnp.stack(new_sf, axis=1), jnp.stack(new_sb, axis=1),
            jnp.stack(new_k, axis=1), jnp.stack(new_v, axis=1))
```
